```python
import math
import jax, jax.numpy as jnp
from jax import lax
import numpy as np

D_MODEL = 1024
BATCH = 4
SEQ = 8192
DEPTH = 1
DEC_BATCH = 8
DEC_SEQ = 16
PAST_LEN = 2048

CHUNK = 64
CONV_W = 4
LRU_WIDTH = 512
LRU_BLOCKS = 8
LRU_BLOCK = LRU_WIDTH // LRU_BLOCKS
LRU_C = 8.0
GDN_HEADS = 4
GDN_DK = 128
GDN_DV = 128
GDN_QK = GDN_HEADS * GDN_DK
GDN_V = GDN_HEADS * GDN_DV
GDN_CONV_CH = 2 * GDN_QK + GDN_V
MIX_WIDTH = LRU_WIDTH + GDN_V
IN_COLS = 2 * LRU_WIDTH + GDN_CONV_CH + GDN_V + 2 * GDN_HEADS
N_GROUPS = 4
EXPERTS_PER_GROUP = 8
N_EXPERTS = N_GROUPS * EXPERTS_PER_GROUP
TOP_K = 2
D_EXPERT = 256
ROW_BLOCK = 128
EPS = 1e-6

kernel_name = 'hybrid_rglru_gdn_hmoe_stream_step'


def rmsnorm(x, g):
    xf = x.astype(jnp.float32)
    y = xf * lax.rsqrt(jnp.mean(xf * xf, axis=-1, keepdims=True) + EPS)
    return (y * g.astype(jnp.float32)).astype(x.dtype)


def l2norm(x):
    return x * lax.rsqrt(jnp.sum(x * x, axis=-1, keepdims=True) + EPS)


def causal_conv(x, prev, w):
    L = x.shape[1]
    xp = jnp.concatenate([prev.astype(x.dtype), x], axis=1)
    out = xp[:, 0:L] * w[0]
    for j in range(1, CONV_W):
        out = out + xp[:, j:j + L] * w[j]
    return out, xp[:, xp.shape[1] - (CONV_W - 1):]


def rglru(xc, h0, wa, ba, wx, bx, lam):
    B, L, _ = xc.shape
    xf = xc.astype(jnp.float32)
    xb = xf.reshape(B, L, LRU_BLOCKS, LRU_BLOCK)
    r = jax.nn.sigmoid(jnp.einsum('blni,nij->blnj', xb, wa.astype(jnp.float32)).reshape(B, L, LRU_WIDTH) + ba.astype(jnp.float32))
    i = jax.nn.sigmoid(jnp.einsum('blni,nij->blnj', xb, wx.astype(jnp.float32)).reshape(B, L, LRU_WIDTH) + bx.astype(jnp.float32))
    log_a = -LRU_C * r * jax.nn.softplus(-lam.astype(jnp.float32))
    a = jnp.exp(log_a)
    b = jnp.sqrt(-jnp.expm1(2.0 * log_a)) * (i * xf)

    def combine(c1, c2):
        a1, b1 = c1
        a2, b2 = c2
        return a1 * a2, a2 * b1 + b2

    a_cum, b_cum = lax.associative_scan(combine, (a, b), axis=1)
    h = a_cum * h0.astype(jnp.float32)[:, None, :] + b_cum
    return h, h[:, -1]


def gdn_chunk(S, q, k, v, g, beta):
    L = q.shape[2]
    gc = jnp.cumsum(g, axis=-1)
    lower = jnp.tril(jnp.ones((L, L), dtype=bool))
    strict = jnp.tril(jnp.ones((L, L), dtype=bool), -1)
    decay = jnp.exp(jnp.where(lower, gc[..., :, None] - gc[..., None, :], -jnp.inf))
    kb = k * beta[..., None]
    lmat = jnp.where(strict, jnp.einsum('bhid,bhjd->bhij', kb, k) * decay, 0.0)
    eye = jnp.eye(L, dtype=q.dtype)
    t_inv = lax.linalg.triangular_solve(eye + lmat, jnp.broadcast_to(eye, lmat.shape),
                                        left_side=True, lower=True, unit_diagonal=True)
    u = t_inv @ (v * beta[..., None])
    w = t_inv @ (kb * jnp.exp(gc)[..., None])
    v_new = u - w @ S
    attn = jnp.einsum('bhid,bhjd->bhij', q, k) * decay
    o = (q * jnp.exp(gc)[..., None]) @ S + attn @ v_new
    g_last = gc[..., -1]
    S_new = S * jnp.exp(g_last)[..., None, None] + jnp.einsum(
        'bhld,bhle->bhde', k * jnp.exp(g_last[..., None] - gc)[..., None], v_new)
    return S_new, o


def gdn_sequence(S0, q, k, v, g, beta):
    B, H, L, _ = q.shape
    if L <= CHUNK:
        S, o = gdn_chunk(S0, q, k, v, g, beta)
        return o, S
    nc = L // CHUNK

    def to_blocks(t):
        t = t.reshape(t.shape[0], t.shape[1], nc, CHUNK, *t.shape[3:])
        return jnp.moveaxis(t, 2, 0)

    S, o = lax.scan(lambda s, c: gdn_chunk(s, *c), S0,
                    (to_blocks(q), to_blocks(k), to_blocks(v), to_blocks(g), to_blocks(beta)))
    o = jnp.moveaxis(o, 0, 2).reshape(B, H, L, GDN_DV)
    return o, S


def mixer(h, conv_l, h_l, conv_g, S_g, w_in, lru_cw, lru_cb, lru_wa, lru_ba, lru_wx, lru_bx,
          lru_lam, gdn_cw, gdn_alog, gdn_dtb, gdn_ng, w_out):
    B, L, _ = h.shape
    f32 = jnp.float32
    proj = h @ w_in
    s1 = LRU_WIDTH
    s2 = 2 * LRU_WIDTH
    s3 = s2 + GDN_CONV_CH
    s4 = s3 + GDN_V
    s5 = s4 + GDN_HEADS
    x_lru = proj[..., :s1]
    y_lru = proj[..., s1:s2]
    qkv = proj[..., s2:s3]
    z = proj[..., s3:s4]
    b_logit = proj[..., s4:s5]
    a_logit = proj[..., s5:]
    xc, new_conv_l = causal_conv(x_lru, conv_l, lru_cw)
    xc = xc + lru_cb
    hs, new_h = rglru(xc, h_l, lru_wa, lru_ba, lru_wx, lru_bx, lru_lam)
    out_a = hs * jax.nn.gelu(y_lru.astype(f32))
    qkv_c, new_conv_g = causal_conv(qkv, conv_g, gdn_cw)
    qkv_c = jax.nn.silu(qkv_c.astype(f32))

    def heads(t, d):
        return t.reshape(B, L, GDN_HEADS, d).transpose(0, 2, 1, 3)

    q = l2norm(heads(qkv_c[..., :GDN_QK], GDN_DK)) * (GDN_DK ** -0.5)
    k = l2norm(heads(qkv_c[..., GDN_QK:2 * GDN_QK], GDN_DK))
    v = heads(qkv_c[..., 2 * GDN_QK:], GDN_DV)
    beta = jax.nn.sigmoid(b_logit.astype(f32)).transpose(0, 2, 1)
    g = (-jnp.exp(gdn_alog.astype(f32)) *
         jax.nn.softplus(a_logit.astype(f32) + gdn_dtb.astype(f32))).transpose(0, 2, 1)
    o, new_S = gdn_sequence(S_g.astype(f32), q, k, v, g, beta)
    o = rmsnorm(o.transpose(0, 2, 1, 3), gdn_ng) * jax.nn.silu(
        z.astype(f32).reshape(B, L, GDN_HEADS, GDN_DV))
    mixed = jnp.concatenate([out_a, o.reshape(B, L, GDN_V)], axis=-1).astype(h.dtype)
    return mixed @ w_out, (new_conv_l, new_h, new_conv_g, new_S)


def expert_dispatch(xf, eidx, wts, w1, w3, w2):
    T, D = xf.shape
    A = T * TOP_K
    e_flat = eidx.reshape(A)
    order = jnp.argsort(e_flat)
    e_sorted = e_flat[order]
    tok_sorted = (order // TOP_K).astype(jnp.int32)
    w_sorted = wts.reshape(A)[order]
    counts = jnp.bincount(e_flat, length=N_EXPERTS)
    padded = (counts + ROW_BLOCK - 1) // ROW_BLOCK * ROW_BLOCK
    pad_end = jnp.cumsum(padded)
    pad_start = pad_end - padded
    start = jnp.cumsum(counts) - counts
    dest = pad_start[e_sorted] + jnp.arange(A) - start[e_sorted]
    n_blocks = -(-(A + N_EXPERTS * (ROW_BLOCK - 1)) // ROW_BLOCK)
    n_rows = n_blocks * ROW_BLOCK
    row_tok = jnp.zeros((n_rows,), jnp.int32).at[dest].set(tok_sorted)
    block_e = jnp.minimum(jnp.searchsorted(pad_end, jnp.arange(n_blocks) * ROW_BLOCK, side='right'),
                          N_EXPERTS - 1)
    xb = xf[row_tok].reshape(n_blocks, ROW_BLOCK, D)

    def run(args):
        e, xblk = args
        return (jax.nn.silu(xblk @ w1[e]) * (xblk @ w3[e])) @ w2[e]

    yb = lax.map(run, (block_e, xb)).reshape(n_rows, D)
    contrib = yb[dest] * w_sorted[:, None].astype(yb.dtype)
    return jnp.zeros((T, D), yb.dtype).at[tok_sorted].add(contrib)


def hier_moe(h, rg_w, rg_b, re_w, re_b, w1, w3, w2):
    B, L, D = h.shape
    xf = h.reshape(B * L, D)
    g_logit = (xf @ rg_w + rg_b).astype(jnp.float32)
    g_prob = jax.nn.softmax(g_logit, axis=-1)
    grp = jnp.argmax(g_logit, axis=-1)
    p_grp = jnp.take_along_axis(g_prob, grp[:, None], axis=-1)
    e_logit = (xf @ re_w + re_b).astype(jnp.float32).reshape(-1, N_GROUPS, EXPERTS_PER_GROUP)
    e_logit = jnp.take_along_axis(e_logit, grp[:, None, None], axis=1)[:, 0]
    top_v, top_i = lax.top_k(e_logit, TOP_K)
    wts = jax.nn.softmax(top_v, axis=-1) * p_grp
    eidx = (grp[:, None] * EXPERTS_PER_GROUP + top_i).astype(jnp.int32)
    return expert_dispatch(xf, eidx, wts, w1, w3, w2).reshape(B, L, D)


def block(x, conv_l, h_l, conv_g, S_g, ln1_g, ln2_g, w_in, lru_cw, lru_cb, lru_wa, lru_ba,
          lru_wx, lru_bx, lru_lam, gdn_cw, gdn_alog, gdn_dtb, gdn_ng, w_out,
          rg_w, rg_b, re_w, re_b, w1, w3, w2):
    mix, states = mixer(rmsnorm(x, ln1_g), conv_l, h_l, conv_g, S_g, w_in, lru_cw, lru_cb,
                        lru_wa, lru_ba, lru_wx, lru_bx, lru_lam, gdn_cw, gdn_alog, gdn_dtb,
                        gdn_ng, w_out)
    x = x + mix
    x = x + hier_moe(rmsnorm(x, ln2_g), rg_w, rg_b, re_w, re_b, w1, w3, w2)
    return x, states


def trunk(x, s_lc, s_lh, s_gc, s_gs, layer_w, final_g):
    new_lc, new_lh, new_gc, new_gs = [], [], [], []
    for l in range(DEPTH):
        x, (a, b, c, d) = block(x, s_lc[l], s_lh[l], s_gc[l], s_gs[l], *[w[l] for w in layer_w])
        new_lc.append(a.astype(s_lc.dtype))
        new_lh.append(b.astype(s_lh.dtype))
        new_gc.append(c.astype(s_gc.dtype))
        new_gs.append(d.astype(s_gs.dtype))
    return (rmsnorm(x, final_g), jnp.stack(new_lc), jnp.stack(new_lh),
            jnp.stack(new_gc), jnp.stack(new_gs))


def setup_inputs(seed: int = 0) -> dict:
    key = jax.random.key(seed)
    ks = jax.random.split(key, 32)
    f32 = jnp.float32

    def nrm(k, shape, scale):
        return jax.random.normal(k, shape, f32) * scale

    a0 = jax.random.uniform(ks[15], (DEPTH, LRU_WIDTH), f32, 0.9, 0.999) ** (1.0 / LRU_C)
    dt = jnp.exp(jax.random.uniform(ks[18], (DEPTH, GDN_HEADS), f32, math.log(1e-3), math.log(0.1)))
    return {
        'x_prompt': nrm(ks[0], (BATCH, SEQ, D_MODEL), 1.0),
        'x_sample': nrm(ks[1], (DEC_BATCH, DEC_SEQ, D_MODEL), 1.0),
        'state_lru_conv': nrm(ks[2], (DEPTH, DEC_BATCH, CONV_W - 1, LRU_WIDTH), 1.0),
        'state_lru_h': nrm(ks[3], (DEPTH, DEC_BATCH, LRU_WIDTH), 0.5),
        'state_gdn_conv': nrm(ks[4], (DEPTH, DEC_BATCH, CONV_W - 1, GDN_CONV_CH), 1.0),
        'state_gdn_S': nrm(ks[5], (DEPTH, DEC_BATCH, GDN_HEADS, GDN_DK, GDN_DV), 0.1),
        'ln1_g': 1.0 + nrm(ks[6], (DEPTH, D_MODEL), 0.02),
        'ln2_g': 1.0 + nrm(ks[7], (DEPTH, D_MODEL), 0.02),
        'w_in': nrm(ks[8], (DEPTH, D_MODEL, IN_COLS), D_MODEL ** -0.5),
        'lru_conv_w': nrm(ks[9], (DEPTH, CONV_W, LRU_WIDTH), CONV_W ** -0.5),
        'lru_conv_b': nrm(ks[10], (DEPTH, LRU_WIDTH), 0.01),
        'lru_wa': nrm(ks[11], (DEPTH, LRU_BLOCKS, LRU_BLOCK, LRU_BLOCK), LRU_BLOCK ** -0.5),
        'lru_ba': nrm(ks[12], (DEPTH, LRU_WIDTH), 0.01),
        'lru_wx': nrm(ks[13], (DEPTH, LRU_BLOCKS, LRU_BLOCK, LRU_BLOCK), LRU_BLOCK ** -0.5),
        'lru_bx': nrm(ks[14], (DEPTH, LRU_WIDTH), 0.01),
        'lru_lambda': jnp.log(a0) - jnp.log1p(-a0),
        'gdn_conv_w': nrm(ks[16], (DEPTH, CONV_W, GDN_CONV_CH), CONV_W ** -0.5),
        'gdn_a_log': jnp.log(jax.random.uniform(ks[17], (DEPTH, GDN_HEADS), f32, 1.0, 16.0)),
        'gdn_dt_bias': dt + jnp.log(-jnp.expm1(-dt)),
        'gdn_norm_g': 1.0 + nrm(ks[19], (DEPTH, GDN_DV), 0.02),
        'w_out': nrm(ks[20], (DEPTH, MIX_WIDTH, D_MODEL), MIX_WIDTH ** -0.5),
        'router_group_w': nrm(ks[21], (DEPTH, D_MODEL, N_GROUPS), D_MODEL ** -0.5),
        'router_group_b': nrm(ks[22], (DEPTH, N_GROUPS), 0.01),
        'router_expert_w': nrm(ks[23], (DEPTH, D_MODEL, N_EXPERTS), D_MODEL ** -0.5),
        'router_expert_b': nrm(ks[24], (DEPTH, N_EXPERTS), 0.01),
        'moe_w1': nrm(ks[25], (DEPTH, N_EXPERTS, D_MODEL, D_EXPERT), D_MODEL ** -0.5),
        'moe_w3': nrm(ks[26], (DEPTH, N_EXPERTS, D_MODEL, D_EXPERT), D_MODEL ** -0.5),
        'moe_w2': nrm(ks[27], (DEPTH, N_EXPERTS, D_EXPERT, D_MODEL), D_EXPERT ** -0.5),
        'final_g': 1.0 + nrm(ks[28], (D_MODEL,), 0.02),
    }


def reference(x_prompt, x_sample, state_lru_conv, state_lru_h, state_gdn_conv, state_gdn_S,
              ln1_g, ln2_g, w_in, lru_conv_w, lru_conv_b, lru_wa, lru_ba, lru_wx, lru_bx,
              lru_lambda, gdn_conv_w, gdn_a_log, gdn_dt_bias, gdn_norm_g, w_out,
              router_group_w, router_group_b, router_expert_w, router_expert_b,
              moe_w1, moe_w3, moe_w2, final_g):
    layer_w = (ln1_g, ln2_g, w_in, lru_conv_w, lru_conv_b, lru_wa, lru_ba, lru_wx, lru_bx,
               lru_lambda, gdn_conv_w, gdn_a_log, gdn_dt_bias, gdn_norm_g, w_out,
               router_group_w, router_group_b, router_expert_w, router_expert_b,
               moe_w1, moe_w3, moe_w2)
    B = x_prompt.shape[0]
    dt = x_prompt.dtype
    y_prompt, p_lc, p_lh, p_gc, p_gs = trunk(
        x_prompt,
        jnp.zeros((DEPTH, B, CONV_W - 1, LRU_WIDTH), dt),
        jnp.zeros((DEPTH, B, LRU_WIDTH), dt),
        jnp.zeros((DEPTH, B, CONV_W - 1, GDN_CONV_CH), dt),
        jnp.zeros((DEPTH, B, GDN_HEADS, GDN_DK, GDN_DV), dt),
        layer_w, final_g)
    y_sample, s_lc, s_lh, s_gc, s_gs = trunk(
        x_sample, state_lru_conv, state_lru_h, state_gdn_conv, state_gdn_S, layer_w, final_g)
    return (y_prompt, y_sample, p_lc, p_lh, p_gc, p_gs, s_lc, s_lh, s_gc, s_gs)
```

```python
import functools

import jax
import jax.numpy as jnp
from jax import lax
from jax.experimental import pallas as pl
from jax.experimental.pallas import tpu as pltpu

F32 = jnp.float32
BF16 = jnp.bfloat16

D_MODEL = 1024
CONV_W = 4
LRU_WIDTH = 512
LRU_BLOCKS = 8
LRU_BLOCK = LRU_WIDTH // LRU_BLOCKS
LRU_C = 8.0
GDN_HEADS = 4
GDN_DK = 128
GDN_DV = 128
GDN_QK = GDN_HEADS * GDN_DK
GDN_V = GDN_HEADS * GDN_DV
GDN_CONV_CH = 2 * GDN_QK + GDN_V
N_GATE = 2 * GDN_HEADS
N_MAIN = 2 * LRU_WIDTH + GDN_CONV_CH + GDN_V
CHUNK = 64
N_GROUPS = 4
EXPERTS_PER_GROUP = 8
N_EXPERTS = N_GROUPS * EXPERTS_PER_GROUP
TOP_K = 2
D_EXPERT = 256
EPS = 1e-6

SUBLANES = 8
LANES = 128
MXU_DIM = 256
VMEM_LIMIT_BYTES = 48 * 1024 * 1024


def _bdot(a, b):
    return jnp.dot(a.astype(BF16), b.astype(BF16), preferred_element_type=F32)


def _bdot_nt(a, b):
    return lax.dot_general(a.astype(BF16), b.astype(BF16), (((1,), (1,)), ((), ())),
                           preferred_element_type=F32)


def _bdot_tn(a, b):
    return lax.dot_general(a.astype(BF16), b.astype(BF16), (((0,), (0,)), ((), ())),
                           preferred_element_type=F32)


def _split2(a):
    hi = a.astype(BF16)
    lo = (a - hi.astype(F32)).astype(BF16)
    return hi, lo


def _dot3(a, b):
    ah, al = _split2(a)
    bh, bl = _split2(b)
    d = functools.partial(jnp.dot, preferred_element_type=F32)
    return d(ah, bh) + (d(ah, bl) + d(al, bh))


def _dot_exact_rhs(a, b01):
    b = b01.astype(BF16)
    a1 = a.astype(BF16)
    r1 = a - a1.astype(F32)
    a2 = r1.astype(BF16)
    a3 = (r1 - a2.astype(F32)).astype(BF16)
    d = functools.partial(jnp.dot, preferred_element_type=F32)
    return d(a1, b) + (d(a2, b) + d(a3, b))


def _dot_exact_lhs(a01, b):
    a = a01.astype(BF16)
    b1 = b.astype(BF16)
    r1 = b - b1.astype(F32)
    b2 = r1.astype(BF16)
    b3 = (r1 - b2.astype(F32)).astype(BF16)
    d = functools.partial(jnp.dot, preferred_element_type=F32)
    return d(a, b1) + (d(a, b2) + d(a, b3))


def _sigmoid(x):
    return 1.0 / (1.0 + jnp.exp(-x))


def _softplus(x):
    return jnp.maximum(x, 0.0) + jnp.log(1.0 + jnp.exp(-jnp.abs(x)))


def _silu(x):
    return x * _sigmoid(x)


def _gelu_tanh(x):
    c = 0.7978845608028654
    return 0.5 * x * (1.0 + jnp.tanh(c * (x + 0.044715 * (x * x * x))))


def _rms_scale(x):
    return lax.rsqrt(jnp.mean(x * x, axis=-1, keepdims=True) + EPS)


def _params(sem):
    return pltpu.CompilerParams(dimension_semantics=sem, vmem_limit_bytes=VMEM_LIMIT_BYTES)


def _mixer_in_kernel(x_ref, g1_ref, wmain_ref, wgc_ref, wgr_ref,
                     lcw_ref, lcb_ref, wgate_ref, ba_ref, bx_ref, lam_ref,
                     gcw_ref, gpc_ref, gpr_ref,
                     lconv0_ref, lh0_ref, gconv0_ref,
                     outa_ref, q_ref, k_ref, v_ref, gz_ref, gcol_ref, grow_ref,
                     nlc_ref, nlh_ref, ngc_ref,
                     xl_buf, xg_buf, hcar, acum, bcum, *, tl):
    lt = pl.program_id(1)
    s1, s2, s3 = LRU_WIDTH, 2 * LRU_WIDTH, 2 * LRU_WIDTH + GDN_CONV_CH

    @pl.when(lt == 0)
    def _():
        xl_buf[0:SUBLANES, :] = lconv0_ref[0]
        xg_buf[0:SUBLANES, :] = gconv0_ref[0]
        hcar[...] = lh0_ref[0]

    @pl.when(lt > 0)
    def _():
        xl_buf[0:SUBLANES, :] = xl_buf[tl:tl + SUBLANES, :]
        xg_buf[0:SUBLANES, :] = xg_buf[tl:tl + SUBLANES, :]

    x = x_ref[0]
    hn = (x * _rms_scale(x) * g1_ref[...]).astype(BF16)

    xl_buf[SUBLANES:SUBLANES + tl, :] = jnp.dot(hn, wmain_ref[:, 0:s1], preferred_element_type=F32)
    base = SUBLANES - (CONV_W - 1)
    xc = lcb_ref[...] + xl_buf[base:base + tl, :] * lcw_ref[0:1, :]
    for j in range(1, CONV_W):
        xc = xc + xl_buf[base + j:base + j + tl, :] * lcw_ref[j:j + 1, :]
    nlc_ref[0] = xl_buf[tl + base:tl + SUBLANES, :]

    half = LRU_WIDTH // 2
    pre = [_bdot(xc[:, s * half:(s + 1) * half], wgate_ref[s]) for s in range(2)]
    r_pre = jnp.concatenate([pre[0][:, :half], pre[1][:, :half]], axis=1)
    i_pre = jnp.concatenate([pre[0][:, half:], pre[1][:, half:]], axis=1)
    r = _sigmoid(r_pre + ba_ref[...])
    ig = _sigmoid(i_pre + bx_ref[...])
    log_a = (-LRU_C) * r * _softplus(-lam_ref[...])
    a = jnp.exp(log_a)
    b = jnp.sqrt(1.0 - jnp.exp(2.0 * log_a)) * (ig * xc)

    ng = tl // SUBLANES
    a3 = a.reshape(ng, SUBLANES, LRU_WIDTH)
    b3 = b.reshape(ng, SUBLANES, LRU_WIDTH)
    row = lax.broadcasted_iota(jnp.int32, (ng, SUBLANES, LRU_WIDTH), 1)
    sh = 1
    while sh < SUBLANES:
        a_s = pltpu.roll(a3, sh, axis=1)
        b_s = pltpu.roll(b3, sh, axis=1)
        m = row >= sh
        b3 = jnp.where(m, a3 * b_s + b3, b3)
        a3 = jnp.where(m, a3 * a_s, a3)
        sh *= 2
    acum[...] = a3.reshape(tl, LRU_WIDTH)
    bcum[...] = b3.reshape(tl, LRU_WIDTH)

    def grp(gi, hprev):
        off = pl.multiple_of(gi * SUBLANES, SUBLANES)
        hg = acum[pl.ds(off, SUBLANES), :] * hprev + bcum[pl.ds(off, SUBLANES), :]
        bcum[pl.ds(off, SUBLANES), :] = hg
        return hg[SUBLANES - 1:SUBLANES, :]

    hlast = lax.fori_loop(0, ng, grp, hcar[...])
    hcar[...] = hlast
    nlh_ref[0] = hlast
    y = jnp.dot(hn, wmain_ref[:, s1:s2], preferred_element_type=F32)
    outa_ref[0] = (bcum[...] * _gelu_tanh(y)).astype(BF16)

    xg_buf[SUBLANES:SUBLANES + tl, :] = jnp.dot(hn, wmain_ref[:, s2:s3], preferred_element_type=F32)
    qkv = xg_buf[base:base + tl, :] * gcw_ref[0:1, :]
    for j in range(1, CONV_W):
        qkv = qkv + xg_buf[base + j:base + j + tl, :] * gcw_ref[j:j + 1, :]
    ngc_ref[0] = xg_buf[tl + base:tl + SUBLANES, :]
    qkv = _silu(qkv)
    for h in range(GDN_HEADS):
        qh = qkv[:, h * GDN_DK:(h + 1) * GDN_DK]
        kh = qkv[:, GDN_QK + h * GDN_DK:GDN_QK + (h + 1) * GDN_DK]
        qs = lax.rsqrt(jnp.sum(qh * qh, axis=-1, keepdims=True) + EPS) * (GDN_DK ** -0.5)
        ks = lax.rsqrt(jnp.sum(kh * kh, axis=-1, keepdims=True) + EPS)
        q_ref[0, :, h * GDN_DK:(h + 1) * GDN_DK] = qh * qs
        k_ref[0, :, h * GDN_DK:(h + 1) * GDN_DK] = kh * ks
    v_ref[0] = qkv[:, 2 * GDN_QK:]
    gz_ref[0] = _silu(jnp.dot(hn, wmain_ref[:, s3:], preferred_element_type=F32))

    gc = jnp.dot(hn, wgc_ref[...], preferred_element_type=F32)
    lane = lax.broadcasted_iota(jnp.int32, gc.shape, 1)
    gcol_ref[0] = jnp.where(lane < GDN_HEADS, _sigmoid(gc),
                            -gpc_ref[0:1, :] * _softplus(gc + gpc_ref[1:2, :]))
    gr = lax.dot_general(wgr_ref[...], hn, (((1,), (1,)), ((), ())), preferred_element_type=F32)
    sub = lax.broadcasted_iota(jnp.int32, gr.shape, 0)
    grow_ref[0] = jnp.where(sub < GDN_HEADS, _sigmoid(gr),
                            -gpr_ref[:, 0:1] * _softplus(gr + gpr_ref[:, 1:2]))


def _mixer_in(x, conv_l8, h_l, conv_g8, wts, tl):
    B, L, _ = x.shape
    nl = L // tl
    const2 = lambda b, l: (0, 0)
    const3 = lambda b, l: (0, 0, 0)
    per_b3 = lambda b, l: (b, 0, 0)
    tile3 = lambda b, l: (b, l, 0)
    in_specs = [
        pl.BlockSpec((1, tl, D_MODEL), tile3),
        pl.BlockSpec((1, D_MODEL), const2),
        pl.BlockSpec((D_MODEL, N_MAIN), const2),
        pl.BlockSpec((D_MODEL, N_GATE), const2),
        pl.BlockSpec((N_GATE, D_MODEL), const2),
        pl.BlockSpec((CONV_W, LRU_WIDTH), const2),
        pl.BlockSpec((1, LRU_WIDTH), const2),
        pl.BlockSpec((2, LRU_WIDTH // 2, LRU_WIDTH), const3),
        pl.BlockSpec((1, LRU_WIDTH), const2),
        pl.BlockSpec((1, LRU_WIDTH), const2),
        pl.BlockSpec((1, LRU_WIDTH), const2),
        pl.BlockSpec((CONV_W, GDN_CONV_CH), const2),
        pl.BlockSpec((2, N_GATE), const2),
        pl.BlockSpec((N_GATE, 2), const2),
        pl.BlockSpec((1, SUBLANES, LRU_WIDTH), per_b3),
        pl.BlockSpec((1, 1, LRU_WIDTH), per_b3),
        pl.BlockSpec((1, SUBLANES, GDN_CONV_CH), per_b3),
    ]
    out_shape = (
        jax.ShapeDtypeStruct((B, L, LRU_WIDTH), BF16),
        jax.ShapeDtypeStruct((B, L, GDN_QK), F32),
        jax.ShapeDtypeStruct((B, L, GDN_QK), F32),
        jax.ShapeDtypeStruct((B, L, GDN_V), F32),
        jax.ShapeDtypeStruct((B, L, GDN_V), F32),
        jax.ShapeDtypeStruct((B, L, N_GATE), F32),
        jax.ShapeDtypeStruct((B, N_GATE, L), F32),
        jax.ShapeDtypeStruct((B, CONV_W - 1, LRU_WIDTH), F32),
        jax.ShapeDtypeStruct((B, 1, LRU_WIDTH), F32),
        jax.ShapeDtypeStruct((B, CONV_W - 1, GDN_CONV_CH), F32),
    )
    out_specs = (
        pl.BlockSpec((1, tl, LRU_WIDTH), tile3),
        pl.BlockSpec((1, tl, GDN_QK), tile3),
        pl.BlockSpec((1, tl, GDN_QK), tile3),
        pl.BlockSpec((1, tl, GDN_V), tile3),
        pl.BlockSpec((1, tl, GDN_V), tile3),
        pl.BlockSpec((1, tl, N_GATE), tile3),
        pl.BlockSpec((1, N_GATE, tl), lambda b, l: (b, 0, l)),
        pl.BlockSpec((1, CONV_W - 1, LRU_WIDTH), per_b3),
        pl.BlockSpec((1, 1, LRU_WIDTH), per_b3),
        pl.BlockSpec((1, CONV_W - 1, GDN_CONV_CH), per_b3),
    )
    scratch = [
        pltpu.VMEM((tl + SUBLANES, LRU_WIDTH), F32),
        pltpu.VMEM((tl + SUBLANES, GDN_CONV_CH), F32),
        pltpu.VMEM((1, LRU_WIDTH), F32),
        pltpu.VMEM((tl, LRU_WIDTH), F32),
        pltpu.VMEM((tl, LRU_WIDTH), F32),
    ]
    return pl.pallas_call(
        functools.partial(_mixer_in_kernel, tl=tl),
        grid=(B, nl), in_specs=in_specs, out_specs=out_specs, out_shape=out_shape,
        scratch_shapes=scratch, compiler_params=_params(("arbitrary", "arbitrary")),
        name="mixer_in",
    )(x, wts["ln1_g"], wts["w_main"], wts["w_gate_col"], wts["w_gate_row"],
      wts["lru_cw"], wts["lru_cb"], wts["lru_wgate"], wts["lru_ba"], wts["lru_bx"], wts["lru_lam"],
      wts["gdn_cw"], wts["gate_par_col"], wts["gate_par_row"],
      conv_l8, h_l, conv_g8)


def _tri_inverse(lm, c):
    r = lax.broadcasted_iota(jnp.int32, (c, c), 0)
    col = lax.broadcasted_iota(jnp.int32, (c, c), 1)
    eye = (r == col).astype(F32)

    def same_block(n):
        sh = n.bit_length() - 1
        return jnp.right_shift(r, sh) == jnp.right_shift(col, sh)

    l8 = jnp.where(same_block(SUBLANES), lm, 0.0)
    l2 = _dot3(l8, l8)
    l4 = _dot3(l2, l2)
    n1 = eye - l8
    p = n1 + _dot3(n1, l2)
    p = p + _dot3(p, l4)
    n = SUBLANES
    while n < c:
        lo = jnp.where(same_block(2 * n) & jnp.logical_not(same_block(n)), lm, 0.0)
        p = p - _dot3(_dot3(p, lo), p)
        n *= 2
    return p


def _gdn_kernel(q_ref, k_ref, v_ref, gz_ref, gcol_ref, grow_ref, s0_ref, ng_ref,
                o_ref, sout_ref, s_scr, *, bb, c):
    ct = pl.program_id(1)

    @pl.when(ct == 0)
    def _():
        s_scr[...] = s0_ref[...]

    r = lax.broadcasted_iota(jnp.int32, (c, c), 0)
    col = lax.broadcasted_iota(jnp.int32, (c, c), 1)
    lower = r >= col
    strict = r > col
    tri_l = lower.astype(F32)
    tri_u = (r <= col).astype(F32)

    for b in range(bb):
        gates_c = gcol_ref[b]
        gates_r = grow_ref[b, 0]
        gcum_c = _dot_exact_lhs(tri_l, gates_c)
        gcum_r = _dot_exact_rhs(gates_r, tri_u)
        for h in range(GDN_HEADS):
            sl = slice(h * GDN_DK, (h + 1) * GDN_DK)
            q = q_ref[b, :, sl]
            k = k_ref[b, :, sl]
            v = v_ref[b, :, sl]
            beta = gates_c[:, h:h + 1]
            gc_c = gcum_c[:, GDN_HEADS + h:GDN_HEADS + h + 1]
            gc_r = gcum_r[GDN_HEADS + h:GDN_HEADS + h + 1, :]
            g_last = gc_c[c - 1:c, :]
            diff = gc_c - gc_r
            decay = jnp.where(lower, jnp.exp(jnp.where(lower, diff, 0.0)), 0.0)
            e_c = jnp.exp(gc_c)
            kb = k * beta
            lmat = jnp.where(strict, _bdot_nt(kb, k) * decay, 0.0)
            t_inv = _tri_inverse(lmat, c)
            uw = _bdot(t_inv, jnp.concatenate([v * beta, kb * e_c], axis=1))
            s = s_scr[b, h]
            v_new = uw[:, :GDN_DV] - _bdot(uw[:, GDN_DV:], s)
            attn = _bdot_nt(q, k) * decay
            o = _bdot(q * e_c, s) + _bdot(attn, v_new)
            k_dec = k * jnp.exp(g_last - gc_c)
            s_scr[b, h] = s * jnp.exp(g_last) + _bdot_tn(k_dec, v_new)
            on = o * _rms_scale(o) * ng_ref[...]
            o_ref[b, :, sl] = (on * gz_ref[b, :, sl]).astype(BF16)

    sout_ref[...] = s_scr[...]


def _gdn(q, k, v, gz, gcol, grow, s0, ng, bb, c):
    B, L, _ = q.shape
    nc = L // c
    tile = lambda i, t: (i, t, 0)
    per_b4 = lambda i, t: (i, 0, 0, 0)
    in_specs = [
        pl.BlockSpec((bb, c, GDN_QK), tile),
        pl.BlockSpec((bb, c, GDN_QK), tile),
        pl.BlockSpec((bb, c, GDN_V), tile),
        pl.BlockSpec((bb, c, GDN_V), tile),
        pl.BlockSpec((bb, c, N_GATE), tile),
        pl.BlockSpec((bb, 1, N_GATE, c), lambda i, t: (i, t, 0, 0)),
        pl.BlockSpec((bb, GDN_HEADS, GDN_DK, GDN_DV), per_b4),
        pl.BlockSpec((1, GDN_DV), lambda i, t: (0, 0)),
    ]
    out_shape = (jax.ShapeDtypeStruct((B, L, GDN_V), BF16),
                 jax.ShapeDtypeStruct((B, GDN_HEADS, GDN_DK, GDN_DV), F32))
    out_specs = (pl.BlockSpec((bb, c, GDN_V), tile),
                 pl.BlockSpec((bb, GDN_HEADS, GDN_DK, GDN_DV), per_b4))
    return pl.pallas_call(
        functools.partial(_gdn_kernel, bb=bb, c=c),
        grid=(B // bb, nc), in_specs=in_specs, out_specs=out_specs, out_shape=out_shape,
        scratch_shapes=[pltpu.VMEM((bb, GDN_HEADS, GDN_DK, GDN_DV), F32)],
        compiler_params=_params(("arbitrary", "arbitrary")),
        name="gdn",
    )(q, k, v, gz, gcol, grow.reshape(B, N_GATE, nc, c).transpose(0, 2, 1, 3), s0, ng)


def _mixer_out_kernel(oa_ref, og_ref, x_ref, wout_ref, g2_ref, wr_ref, br_ref,
                      x1_ref, h2_ref, ew_ref, ei_ref):
    mix = (jnp.dot(oa_ref[...], wout_ref[0:LRU_WIDTH, :], preferred_element_type=F32)
           + jnp.dot(og_ref[...], wout_ref[LRU_WIDTH:, :], preferred_element_type=F32))
    x1 = x_ref[...] + mix
    x1_ref[...] = x1
    h2 = x1 * _rms_scale(x1) * g2_ref[...]
    h2_ref[...] = h2
    logits = _dot3(h2, wr_ref[...]) + br_ref[...]
    lane = lax.broadcasted_iota(jnp.int32, logits.shape, 1)
    lane_f = lane.astype(F32)
    neg = jnp.float32(-jnp.inf)
    big = jnp.float32(1e9)
    gmask = lane < N_GROUPS
    gl = jnp.where(gmask, logits, neg)
    gmax = jnp.max(gl, axis=1, keepdims=True)
    grp = jnp.min(jnp.where(gl == gmax, lane_f, big), axis=1, keepdims=True)
    p_grp = 1.0 / jnp.sum(jnp.where(gmask, jnp.exp(gl - gmax), 0.0), axis=1, keepdims=True)
    egrp = jnp.right_shift(lane - N_GROUPS, EXPERTS_PER_GROUP.bit_length() - 1).astype(F32)
    sel = (lane >= N_GROUPS) & (lane < N_GROUPS + N_EXPERTS) & (egrp == grp)
    el = jnp.where(sel, logits, neg)
    v1 = jnp.max(el, axis=1, keepdims=True)
    i1 = jnp.min(jnp.where(el == v1, lane_f, big), axis=1, keepdims=True)
    el2 = jnp.where(lane_f == i1, neg, el)
    v2 = jnp.max(el2, axis=1, keepdims=True)
    i2 = jnp.min(jnp.where(el2 == v2, lane_f, big), axis=1, keepdims=True)
    e2 = jnp.exp(v2 - v1)
    den = 1.0 / (1.0 + e2)
    w1 = den * p_grp
    w2 = (e2 * den) * p_grp
    ew_ref[...] = jnp.where(lane == 0, w1, jnp.where(lane == 1, w2, 0.0))
    ei_ref[...] = jnp.where(lane == 0, i1 - N_GROUPS,
                            jnp.where(lane == 1, i2 - N_GROUPS, 0.0)).astype(jnp.int32)


def _mixer_out(oa, og, x, wts, tl):
    T = x.shape[0]
    tile = lambda i: (i, 0)
    const = lambda i: (0, 0)
    in_specs = [
        pl.BlockSpec((tl, LRU_WIDTH), tile),
        pl.BlockSpec((tl, GDN_V), tile),
        pl.BlockSpec((tl, D_MODEL), tile),
        pl.BlockSpec((D_MODEL, D_MODEL), const),
        pl.BlockSpec((1, D_MODEL), const),
        pl.BlockSpec((D_MODEL, LANES), const),
        pl.BlockSpec((1, LANES), const),
    ]
    out_shape = (jax.ShapeDtypeStruct((T, D_MODEL), F32),
                 jax.ShapeDtypeStruct((T, D_MODEL), F32),
                 jax.ShapeDtypeStruct((T, LANES), F32),
                 jax.ShapeDtypeStruct((T, LANES), jnp.int32))
    out_specs = (pl.BlockSpec((tl, D_MODEL), tile), pl.BlockSpec((tl, D_MODEL), tile),
                 pl.BlockSpec((tl, LANES), tile), pl.BlockSpec((tl, LANES), tile))
    return pl.pallas_call(
        _mixer_out_kernel, grid=(T // tl,), in_specs=in_specs, out_specs=out_specs,
        out_shape=out_shape, compiler_params=_params(("arbitrary",)), name="mixer_out",
    )(oa, og, x, wts["w_out"], wts["ln2_g"], wts["w_router"], wts["b_router"])


def _row_copy(src, src_row, dst, dst_row, sem):
    return pltpu.make_async_copy(src.at[pl.ds(src_row, 1), :], dst.at[pl.ds(dst_row, 1), :], sem)


def _moe_kernel(be_ref, tok_ref, slot_ref, h2_hbm, w1_ref, w3_ref, w2_ref, ya_hbm,
                xbuf, ybuf, sem_g, sem_s, *, rb):
    del be_ref

    def gather_start(r, carry):
        _row_copy(h2_hbm, tok_ref[0, 0, r], xbuf, r, sem_g).start()
        return carry

    def gather_wait(r, carry):
        _row_copy(h2_hbm, tok_ref[0, 0, r], xbuf, r, sem_g).wait()
        return carry

    lax.fori_loop(0, rb, gather_start, 0)
    lax.fori_loop(0, rb, gather_wait, 0)
    xb = xbuf[...].astype(BF16)
    hmid = _silu(jnp.dot(xb, w1_ref[0], preferred_element_type=F32)) * jnp.dot(
        xb, w3_ref[0], preferred_element_type=F32)
    ybuf[...] = jnp.dot(hmid.astype(BF16), w2_ref[0], preferred_element_type=F32)

    def scatter_start(r, carry):
        _row_copy(ybuf, r, ya_hbm, slot_ref[0, 0, r], sem_s).start()
        return carry

    def scatter_wait(r, carry):
        _row_copy(ybuf, r, ya_hbm, slot_ref[0, 0, r], sem_s).wait()
        return carry

    lax.fori_loop(0, rb, scatter_start, 0)
    lax.fori_loop(0, rb, scatter_wait, 0)


def _moe(block_e, row_tok, row_slot, h2, wts, rb):
    n_blocks = block_e.shape[0]
    n_rows = n_blocks * rb
    idx_spec = pl.BlockSpec((1, 1, rb), lambda i, be: (i, 0, 0), memory_space=pltpu.SMEM)
    grid_spec = pltpu.PrefetchScalarGridSpec(
        num_scalar_prefetch=1, grid=(n_blocks,),
        in_specs=[
            idx_spec, idx_spec,
            pl.BlockSpec(memory_space=pl.ANY),
            pl.BlockSpec((1, D_MODEL, D_EXPERT), lambda i, be: (be[i], 0, 0)),
            pl.BlockSpec((1, D_MODEL, D_EXPERT), lambda i, be: (be[i], 0, 0)),
            pl.BlockSpec((1, D_EXPERT, D_MODEL), lambda i, be: (be[i], 0, 0)),
        ],
        out_specs=pl.BlockSpec(memory_space=pl.ANY),
        scratch_shapes=[pltpu.VMEM((rb, D_MODEL), F32), pltpu.VMEM((rb, D_MODEL), F32),
                        pltpu.SemaphoreType.DMA(()), pltpu.SemaphoreType.DMA(())],
    )
    return pl.pallas_call(
        functools.partial(_moe_kernel, rb=rb), grid_spec=grid_spec,
        out_shape=jax.ShapeDtypeStruct((n_rows, D_MODEL), F32),
        compiler_params=_params(("arbitrary",)), name="moe",
    )(block_e, row_tok.reshape(n_blocks, 1, rb), row_slot.reshape(n_blocks, 1, rb), h2,
      wts["moe_w1"], wts["moe_w3"], wts["moe_w2"])


def _dispatch_plan(eidx, rb):
    T = eidx.shape[0]
    A = T * TOP_K
    n_blocks = -(-(A + N_EXPERTS * (rb - 1)) // rb)
    n_blocks += n_blocks % 2
    n_rows = n_blocks * rb
    e_flat = eidx.reshape(A)
    order = jnp.argsort(e_flat).astype(jnp.int32)
    e_sorted = e_flat[order]
    counts = jnp.bincount(e_flat, length=N_EXPERTS).astype(jnp.int32)
    padded = (counts + rb - 1) // rb * rb
    pad_end = jnp.cumsum(padded)
    pad_start = pad_end - padded
    start = jnp.cumsum(counts) - counts
    dest = pad_start[e_sorted] + jnp.arange(A, dtype=jnp.int32) - start[e_sorted]
    row_asg = jnp.full((n_rows,), -1, jnp.int32).at[dest].set(order)
    is_pad = row_asg < 0
    pad_rank = jnp.cumsum(is_pad.astype(jnp.int32)) - 1
    row_slot = jnp.where(is_pad, A + pad_rank, row_asg)
    row_tok = jnp.where(is_pad, 0, row_asg // TOP_K)
    block_e = jnp.minimum(
        jnp.searchsorted(pad_end, jnp.arange(n_blocks, dtype=jnp.int32) * rb, side="right"),
        N_EXPERTS - 1).astype(jnp.int32)
    return block_e, row_tok, row_slot


def _combine_kernel(x1_ref, ya_ref, ew_ref, gf_ref, y_ref):
    w = ew_ref[...]
    moe = w[:, 0:1] * ya_ref[:, 0:D_MODEL] + w[:, 1:2] * ya_ref[:, D_MODEL:]
    x2 = x1_ref[...] + moe
    y_ref[...] = x2 * _rms_scale(x2) * gf_ref[...]


def _combine(x1, ya2, ew, final_g, tl):
    T = x1.shape[0]
    tile = lambda i: (i, 0)
    return pl.pallas_call(
        _combine_kernel, grid=(T // tl,),
        in_specs=[pl.BlockSpec((tl, D_MODEL), tile), pl.BlockSpec((tl, TOP_K * D_MODEL), tile),
                  pl.BlockSpec((tl, LANES), tile), pl.BlockSpec((1, D_MODEL), lambda i: (0, 0))],
        out_specs=pl.BlockSpec((tl, D_MODEL), tile),
        out_shape=jax.ShapeDtypeStruct((T, D_MODEL), F32),
        compiler_params=_params(("arbitrary",)), name="combine",
    )(x1, ya2, ew, final_g)


def _prep_weights(ln1_g, ln2_g, w_in, lru_conv_w, lru_conv_b, lru_wa, lru_ba, lru_wx, lru_bx,
                  lru_lambda, gdn_conv_w, gdn_a_log, gdn_dt_bias, gdn_norm_g, w_out,
                  router_group_w, router_group_b, router_expert_w, router_expert_b,
                  moe_w1, moe_w3, moe_w2, final_g):
    w_in0 = w_in[0]
    half = LRU_WIDTH // 2
    per_half = half // LRU_BLOCK

    def block_diag(w):
        w4 = w.reshape(2, per_half, LRU_BLOCK, LRU_BLOCK)
        eye = jnp.eye(per_half, dtype=w.dtype)
        return jnp.einsum("snij,nm->snimj", w4, eye).reshape(2, half, half)

    wgate = jnp.concatenate([block_diag(lru_wa[0]), block_diag(lru_wx[0])], axis=2).astype(BF16)
    zeros4 = jnp.zeros((GDN_HEADS,), F32)
    neg_a = jnp.concatenate([zeros4, jnp.exp(gdn_a_log[0].astype(F32))])
    dtb = jnp.concatenate([zeros4, gdn_dt_bias[0].astype(F32)])
    gate_par = jnp.stack([neg_a, dtb])
    w_router = jnp.zeros((D_MODEL, LANES), F32)
    w_router = w_router.at[:, :N_GROUPS].set(router_group_w[0])
    w_router = w_router.at[:, N_GROUPS:N_GROUPS + N_EXPERTS].set(router_expert_w[0])
    b_router = jnp.zeros((1, LANES), F32)
    b_router = b_router.at[0, :N_GROUPS].set(router_group_b[0])
    b_router = b_router.at[0, N_GROUPS:N_GROUPS + N_EXPERTS].set(router_expert_b[0])
    return {
        "ln1_g": ln1_g[0].reshape(1, D_MODEL), "ln2_g": ln2_g[0].reshape(1, D_MODEL),
        "w_main": w_in0[:, :N_MAIN].astype(BF16),
        "w_gate_col": w_in0[:, N_MAIN:].astype(BF16),
        "w_gate_row": w_in0[:, N_MAIN:].T.astype(BF16),
        "lru_cw": lru_conv_w[0], "lru_cb": lru_conv_b[0].reshape(1, LRU_WIDTH),
        "lru_wgate": wgate,
        "lru_ba": lru_ba[0].reshape(1, LRU_WIDTH), "lru_bx": lru_bx[0].reshape(1, LRU_WIDTH),
        "lru_lam": lru_lambda[0].reshape(1, LRU_WIDTH),
        "gdn_cw": gdn_conv_w[0], "gate_par_col": gate_par, "gate_par_row": gate_par.T,
        "gdn_ng": gdn_norm_g[0].reshape(1, GDN_DV),
        "w_out": w_out[0].astype(BF16),
        "w_router": w_router, "b_router": b_router,
        "moe_w1": moe_w1[0].astype(BF16), "moe_w3": moe_w3[0].astype(BF16),
        "moe_w2": moe_w2[0].astype(BF16),
        "final_g": final_g.reshape(1, D_MODEL),
    }


def _pad_conv_state(s):
    return jnp.pad(s, ((0, 0), (SUBLANES - (CONV_W - 1), 0), (0, 0)))


def _trunk(x, conv_l, h_l, conv_g, s_g, wts, tl_in, bb, tl_tok, rb):
    B, L, _ = x.shape
    T = B * L
    c = min(CHUNK, L)
    oa, q, k, v, gz, gcol, grow, nlc, nlh, ngc = _mixer_in(
        x, _pad_conv_state(conv_l), h_l.reshape(B, 1, LRU_WIDTH), _pad_conv_state(conv_g), wts, tl_in)
    og, ns = _gdn(q, k, v, gz, gcol, grow, s_g, wts["gdn_ng"], bb, c)
    x1, h2, ew, ei = _mixer_out(oa.reshape(T, LRU_WIDTH), og.reshape(T, GDN_V),
                                x.reshape(T, D_MODEL), wts, tl_tok)
    block_e, row_tok, row_slot = _dispatch_plan(ei[:, :TOP_K], rb)
    ya = _moe(block_e, row_tok, row_slot, h2, wts, rb)
    y = _combine(x1, ya.reshape(-1, TOP_K * D_MODEL), ew, wts["final_g"], tl_tok)
    return (y.reshape(B, L, D_MODEL), nlc[None], nlh.reshape(1, B, LRU_WIDTH), ngc[None], ns[None])


def kernel(x_prompt, x_sample, state_lru_conv, state_lru_h, state_gdn_conv, state_gdn_S, ln1_g, ln2_g, w_in, lru_conv_w, lru_conv_b, lru_wa, lru_ba, lru_wx, lru_bx, lru_lambda, gdn_conv_w, gdn_a_log, gdn_dt_bias, gdn_norm_g, w_out, router_group_w, router_group_b, router_expert_w, router_expert_b, moe_w1, moe_w3, moe_w2, final_g):
    wts = _prep_weights(ln1_g, ln2_g, w_in, lru_conv_w, lru_conv_b, lru_wa, lru_ba, lru_wx, lru_bx,
                        lru_lambda, gdn_conv_w, gdn_a_log, gdn_dt_bias, gdn_norm_g, w_out,
                        router_group_w, router_group_b, router_expert_w, router_expert_b,
                        moe_w1, moe_w3, moe_w2, final_g)
    B, L, _ = x_prompt.shape
    dt = x_prompt.dtype
    y_p, p_lc, p_lh, p_gc, p_gs = _trunk(
        x_prompt,
        jnp.zeros((B, CONV_W - 1, LRU_WIDTH), dt), jnp.zeros((B, LRU_WIDTH), dt),
        jnp.zeros((B, CONV_W - 1, GDN_CONV_CH), dt), jnp.zeros((B, GDN_HEADS, GDN_DK, GDN_DV), dt),
        wts, tl_in=256, bb=4, tl_tok=512, rb=256)
    Bs, Ls, _ = x_sample.shape
    y_s, s_lc, s_lh, s_gc, s_gs = _trunk(
        x_sample, state_lru_conv[0], state_lru_h[0], state_gdn_conv[0], state_gdn_S[0],
        wts, tl_in=Ls, bb=2, tl_tok=Bs * Ls, rb=128)
    return (y_p, y_s, p_lc, p_lh, p_gc, p_gs, s_lc, s_lh, s_gc, s_gs)
```

```python
import functools

import jax
import jax.numpy as jnp
from jax import lax
from jax.experimental import pallas as pl
from jax.experimental.pallas import tpu as pltpu

F32 = jnp.float32
BF16 = jnp.bfloat16

D_MODEL = 1024
CONV_W = 4
LRU_WIDTH = 512
LRU_BLOCKS = 8
LRU_BLOCK = LRU_WIDTH // LRU_BLOCKS
LRU_C = 8.0
GDN_HEADS = 4
GDN_DK = 128
GDN_DV = 128
GDN_QK = GDN_HEADS * GDN_DK
GDN_V = GDN_HEADS * GDN_DV
GDN_CONV_CH = 2 * GDN_QK + GDN_V
N_GATE = 2 * GDN_HEADS
N_MAIN = 2 * LRU_WIDTH + GDN_CONV_CH + GDN_V
CHUNK = 64
N_GROUPS = 4
EXPERTS_PER_GROUP = 8
N_EXPERTS = N_GROUPS * EXPERTS_PER_GROUP
TOP_K = 2
D_EXPERT = 256
EPS = 1e-6

SUBLANES = 8
LANES = 128
MXU_DIM = 256
VMEM_LIMIT_BYTES = 48 * 1024 * 1024


def _bdot(a, b):
    return jnp.dot(a.astype(BF16), b.astype(BF16), preferred_element_type=F32)


def _bdot_nt(a, b):
    return lax.dot_general(a.astype(BF16), b.astype(BF16), (((1,), (1,)), ((), ())),
                           preferred_element_type=F32)


def _bdot_tn(a, b):
    return lax.dot_general(a.astype(BF16), b.astype(BF16), (((0,), (0,)), ((), ())),
                           preferred_element_type=F32)


def _split2(a):
    hi = a.astype(BF16)
    lo = (a - hi.astype(F32)).astype(BF16)
    return hi, lo


def _dot3(a, b):
    ah, al = _split2(a)
    bh, bl = _split2(b)
    d = functools.partial(jnp.dot, preferred_element_type=F32)
    return d(ah, bh) + (d(ah, bl) + d(al, bh))


def _dot_exact_rhs(a, b01):
    b = b01.astype(BF16)
    a1 = a.astype(BF16)
    r1 = a - a1.astype(F32)
    a2 = r1.astype(BF16)
    a3 = (r1 - a2.astype(F32)).astype(BF16)
    d = functools.partial(jnp.dot, preferred_element_type=F32)
    return d(a1, b) + (d(a2, b) + d(a3, b))


def _dot_exact_lhs(a01, b):
    a = a01.astype(BF16)
    b1 = b.astype(BF16)
    r1 = b - b1.astype(F32)
    b2 = r1.astype(BF16)
    b3 = (r1 - b2.astype(F32)).astype(BF16)
    d = functools.partial(jnp.dot, preferred_element_type=F32)
    return d(a, b1) + (d(a, b2) + d(a, b3))


def _sigmoid(x):
    return 1.0 / (1.0 + jnp.exp(-x))


def _softplus(x):
    return jnp.maximum(x, 0.0) + jnp.log(1.0 + jnp.exp(-jnp.abs(x)))


def _silu(x):
    return x * _sigmoid(x)


def _gelu_tanh(x):
    c = 0.7978845608028654
    return 0.5 * x * (1.0 + jnp.tanh(c * (x + 0.044715 * (x * x * x))))


def _rms_scale(x):
    return lax.rsqrt(jnp.mean(x * x, axis=-1, keepdims=True) + EPS)


def _params(sem):
    return pltpu.CompilerParams(dimension_semantics=sem, vmem_limit_bytes=VMEM_LIMIT_BYTES)


def _mixer_in_kernel(x_ref, g1_ref, wmain_ref, wgc_ref, wgr_ref,
                     lcw_ref, lcb_ref, wgate_ref, ba_ref, bx_ref, lam_ref,
                     gcw_ref, gpc_ref, gpr_ref,
                     lconv0_ref, lh0_ref, gconv0_ref,
                     outa_ref, q_ref, k_ref, v_ref, gz_ref, gcol_ref, grow_ref,
                     nlc_ref, nlh_ref, ngc_ref,
                     xl_buf, xg_buf, hcar, acum, bcum, *, tl):
    lt = pl.program_id(1)
    s1, s2, s3 = LRU_WIDTH, 2 * LRU_WIDTH, 2 * LRU_WIDTH + GDN_CONV_CH

    @pl.when(lt == 0)
    def _():
        xl_buf[0:SUBLANES, :] = lconv0_ref[0]
        xg_buf[0:SUBLANES, :] = gconv0_ref[0]
        hcar[...] = lh0_ref[0]

    @pl.when(lt > 0)
    def _():
        xl_buf[0:SUBLANES, :] = xl_buf[tl:tl + SUBLANES, :]
        xg_buf[0:SUBLANES, :] = xg_buf[tl:tl + SUBLANES, :]

    x = x_ref[0]
    hn = (x * _rms_scale(x) * g1_ref[...]).astype(BF16)

    xl_buf[SUBLANES:SUBLANES + tl, :] = jnp.dot(hn, wmain_ref[:, 0:s1], preferred_element_type=F32)
    base = SUBLANES - (CONV_W - 1)
    xc = lcb_ref[...] + xl_buf[base:base + tl, :] * lcw_ref[0:1, :]
    for j in range(1, CONV_W):
        xc = xc + xl_buf[base + j:base + j + tl, :] * lcw_ref[j:j + 1, :]
    nlc_ref[0] = xl_buf[tl + base:tl + SUBLANES, :]

    half = LRU_WIDTH // 2
    pre = [_bdot(xc[:, s * half:(s + 1) * half], wgate_ref[s]) for s in range(2)]
    r_pre = jnp.concatenate([pre[0][:, :half], pre[1][:, :half]], axis=1)
    i_pre = jnp.concatenate([pre[0][:, half:], pre[1][:, half:]], axis=1)
    r = _sigmoid(r_pre + ba_ref[...])
    ig = _sigmoid(i_pre + bx_ref[...])
    log_a = (-LRU_C) * r * _softplus(-lam_ref[...])
    a = jnp.exp(log_a)
    b = jnp.sqrt(1.0 - jnp.exp(2.0 * log_a)) * (ig * xc)

    ng = tl // SUBLANES
    a3 = a.reshape(ng, SUBLANES, LRU_WIDTH)
    b3 = b.reshape(ng, SUBLANES, LRU_WIDTH)
    row = lax.broadcasted_iota(jnp.int32, (ng, SUBLANES, LRU_WIDTH), 1)
    sh = 1
    while sh < SUBLANES:
        a_s = pltpu.roll(a3, sh, axis=1)
        b_s = pltpu.roll(b3, sh, axis=1)
        m = row >= sh
        b3 = jnp.where(m, a3 * b_s + b3, b3)
        a3 = jnp.where(m, a3 * a_s, a3)
        sh *= 2
    acum[...] = a3.reshape(tl, LRU_WIDTH)
    bcum[...] = b3.reshape(tl, LRU_WIDTH)

    def grp(gi, hprev):
        off = pl.multiple_of(gi * SUBLANES, SUBLANES)
        hg = acum[pl.ds(off, SUBLANES), :] * hprev + bcum[pl.ds(off, SUBLANES), :]
        bcum[pl.ds(off, SUBLANES), :] = hg
        return hg[SUBLANES - 1:SUBLANES, :]

    hlast = lax.fori_loop(0, ng, grp, hcar[...])
    hcar[...] = hlast
    nlh_ref[0] = hlast
    y = jnp.dot(hn, wmain_ref[:, s1:s2], preferred_element_type=F32)
    outa_ref[0] = (bcum[...] * _gelu_tanh(y)).astype(BF16)

    xg_buf[SUBLANES:SUBLANES + tl, :] = jnp.dot(hn, wmain_ref[:, s2:s3], preferred_element_type=F32)
    qkv = xg_buf[base:base + tl, :] * gcw_ref[0:1, :]
    for j in range(1, CONV_W):
        qkv = qkv + xg_buf[base + j:base + j + tl, :] * gcw_ref[j:j + 1, :]
    ngc_ref[0] = xg_buf[tl + base:tl + SUBLANES, :]
    qkv = _silu(qkv)
    for h in range(GDN_HEADS):
        qh = qkv[:, h * GDN_DK:(h + 1) * GDN_DK]
        kh = qkv[:, GDN_QK + h * GDN_DK:GDN_QK + (h + 1) * GDN_DK]
        qs = lax.rsqrt(jnp.sum(qh * qh, axis=-1, keepdims=True) + EPS) * (GDN_DK ** -0.5)
        ks = lax.rsqrt(jnp.sum(kh * kh, axis=-1, keepdims=True) + EPS)
        q_ref[0, :, h * GDN_DK:(h + 1) * GDN_DK] = qh * qs
        k_ref[0, :, h * GDN_DK:(h + 1) * GDN_DK] = kh * ks
    v_ref[0] = qkv[:, 2 * GDN_QK:]
    gz_ref[0] = _silu(jnp.dot(hn, wmain_ref[:, s3:], preferred_element_type=F32))

    gc = jnp.dot(hn, wgc_ref[...], preferred_element_type=F32)
    lane = lax.broadcasted_iota(jnp.int32, gc.shape, 1)
    gcol_ref[0] = jnp.where(lane < GDN_HEADS, _sigmoid(gc),
                            -gpc_ref[0:1, :] * _softplus(gc + gpc_ref[1:2, :]))
    gr = lax.dot_general(wgr_ref[...], hn, (((1,), (1,)), ((), ())), preferred_element_type=F32)
    sub = lax.broadcasted_iota(jnp.int32, gr.shape, 0)
    grow_ref[0] = jnp.where(sub < GDN_HEADS, _sigmoid(gr),
                            -gpr_ref[:, 0:1] * _softplus(gr + gpr_ref[:, 1:2]))


def _mixer_in(x, conv_l8, h_l, conv_g8, wts, tl):
    B, L, _ = x.shape
    nl = L // tl
    const2 = lambda b, l: (0, 0)
    const3 = lambda b, l: (0, 0, 0)
    per_b3 = lambda b, l: (b, 0, 0)
    tile3 = lambda b, l: (b, l, 0)
    in_specs = [
        pl.BlockSpec((1, tl, D_MODEL), tile3),
        pl.BlockSpec((1, D_MODEL), const2),
        pl.BlockSpec((D_MODEL, N_MAIN), const2),
        pl.BlockSpec((D_MODEL, N_GATE), const2),
        pl.BlockSpec((N_GATE, D_MODEL), const2),
        pl.BlockSpec((CONV_W, LRU_WIDTH), const2),
        pl.BlockSpec((1, LRU_WIDTH), const2),
        pl.BlockSpec((2, LRU_WIDTH // 2, LRU_WIDTH), const3),
        pl.BlockSpec((1, LRU_WIDTH), const2),
        pl.BlockSpec((1, LRU_WIDTH), const2),
        pl.BlockSpec((1, LRU_WIDTH), const2),
        pl.BlockSpec((CONV_W, GDN_CONV_CH), const2),
        pl.BlockSpec((2, N_GATE), const2),
        pl.BlockSpec((N_GATE, 2), const2),
        pl.BlockSpec((1, SUBLANES, LRU_WIDTH), per_b3),
        pl.BlockSpec((1, 1, LRU_WIDTH), per_b3),
        pl.BlockSpec((1, SUBLANES, GDN_CONV_CH), per_b3),
    ]
    out_shape = (
        jax.ShapeDtypeStruct((B, L, LRU_WIDTH), BF16),
        jax.ShapeDtypeStruct((B, L, GDN_QK), F32),
        jax.ShapeDtypeStruct((B, L, GDN_QK), F32),
        jax.ShapeDtypeStruct((B, L, GDN_V), F32),
        jax.ShapeDtypeStruct((B, L, GDN_V), F32),
        jax.ShapeDtypeStruct((B, L, N_GATE), F32),
        jax.ShapeDtypeStruct((B, N_GATE, L), F32),
        jax.ShapeDtypeStruct((B, CONV_W - 1, LRU_WIDTH), F32),
        jax.ShapeDtypeStruct((B, 1, LRU_WIDTH), F32),
        jax.ShapeDtypeStruct((B, CONV_W - 1, GDN_CONV_CH), F32),
    )
    out_specs = (
        pl.BlockSpec((1, tl, LRU_WIDTH), tile3),
        pl.BlockSpec((1, tl, GDN_QK), tile3),
        pl.BlockSpec((1, tl, GDN_QK), tile3),
        pl.BlockSpec((1, tl, GDN_V), tile3),
        pl.BlockSpec((1, tl, GDN_V), tile3),
        pl.BlockSpec((1, tl, N_GATE), tile3),
        pl.BlockSpec((1, N_GATE, tl), lambda b, l: (b, 0, l)),
        pl.BlockSpec((1, CONV_W - 1, LRU_WIDTH), per_b3),
        pl.BlockSpec((1, 1, LRU_WIDTH), per_b3),
        pl.BlockSpec((1, CONV_W - 1, GDN_CONV_CH), per_b3),
    )
    scratch = [
        pltpu.VMEM((tl + SUBLANES, LRU_WIDTH), F32),
        pltpu.VMEM((tl + SUBLANES, GDN_CONV_CH), F32),
        pltpu.VMEM((1, LRU_WIDTH), F32),
        pltpu.VMEM((tl, LRU_WIDTH), F32),
        pltpu.VMEM((tl, LRU_WIDTH), F32),
    ]
    return pl.pallas_call(
        functools.partial(_mixer_in_kernel, tl=tl),
        grid=(B, nl), in_specs=in_specs, out_specs=out_specs, out_shape=out_shape,
        scratch_shapes=scratch, compiler_params=_params(("arbitrary", "arbitrary")),
        name="mixer_in",
    )(x, wts["ln1_g"], wts["w_main"], wts["w_gate_col"], wts["w_gate_row"],
      wts["lru_cw"], wts["lru_cb"], wts["lru_wgate"], wts["lru_ba"], wts["lru_bx"], wts["lru_lam"],
      wts["gdn_cw"], wts["gate_par_col"], wts["gate_par_row"],
      conv_l8, h_l, conv_g8)


def _tri_inverse_all(lms, c):
    r = lax.broadcasted_iota(jnp.int32, (c, c), 0)
    col = lax.broadcasted_iota(jnp.int32, (c, c), 1)
    eye = (r == col).astype(F32)

    def same_block(n):
        sh = n.bit_length() - 1
        return jnp.right_shift(r, sh) == jnp.right_shift(col, sh)

    blk8 = same_block(SUBLANES)
    l8 = [jnp.where(blk8, lm, 0.0) for lm in lms]
    l2 = [_dot3(a, a) for a in l8]
    p = [(eye - a) + _dot3(eye - a, b) for a, b in zip(l8, l2)]
    l4 = [_dot3(a, a) for a in l2]
    p = [a + _dot3(a, b) for a, b in zip(p, l4)]
    n = SUBLANES
    while n < c:
        off = same_block(2 * n) & jnp.logical_not(same_block(n))
        lo = [jnp.where(off, lm, 0.0) for lm in lms]
        t = [_dot3(a, b) for a, b in zip(p, lo)]
        p = [a - _dot3(b, a) for a, b in zip(p, t)]
        n *= 2
    return p


def _gdn_kernel(q_ref, k_ref, v_ref, gz_ref, gcol_ref, grow_ref, s0_ref, ng_ref,
                o_ref, sout_ref, s_scr, *, bb, c):
    ct = pl.program_id(1)

    @pl.when(ct == 0)
    def _():
        s_scr[...] = s0_ref[...]

    r = lax.broadcasted_iota(jnp.int32, (c, c), 0)
    col = lax.broadcasted_iota(jnp.int32, (c, c), 1)
    lower = r >= col
    strict = r > col
    tri_l = lower.astype(F32)
    tri_u = (r <= col).astype(F32)

    chains = [(b, h) for b in range(bb) for h in range(GDN_HEADS)]

    def hs(h):
        return slice(h * GDN_DK, (h + 1) * GDN_DK)

    gates_c = [gcol_ref[b] for b in range(bb)]
    gcum_c = [_dot_exact_lhs(tri_l, g) for g in gates_c]
    gcum_r = [_dot_exact_rhs(grow_ref[b, 0], tri_u) for b in range(bb)]
    beta = [gates_c[b][:, h:h + 1] for b, h in chains]
    gc_c = [gcum_c[b][:, GDN_HEADS + h:GDN_HEADS + h + 1] for b, h in chains]
    gc_r = [gcum_r[b][GDN_HEADS + h:GDN_HEADS + h + 1, :] for b, h in chains]
    decay = [jnp.where(lower, jnp.exp(jnp.where(lower, a - b, 0.0)), 0.0) for a, b in zip(gc_c, gc_r)]
    e_c = [jnp.exp(a) for a in gc_c]
    kb = [k_ref[b, :, hs(h)] * bt for (b, h), bt in zip(chains, beta)]
    lmat = [jnp.where(strict, _bdot_nt(a, k_ref[b, :, hs(h)]) * d, 0.0)
            for (b, h), a, d in zip(chains, kb, decay)]
    t_inv = _tri_inverse_all(lmat, c)
    uw = [_bdot(t, jnp.concatenate([v_ref[b, :, hs(h)] * bt, a * e], axis=1))
          for (b, h), t, bt, a, e in zip(chains, t_inv, beta, kb, e_c)]
    v_new = [x[:, :GDN_DV] - _bdot(x[:, GDN_DV:], s_scr[b, h]) for (b, h), x in zip(chains, uw)]
    attn = [_bdot_nt(q_ref[b, :, hs(h)], k_ref[b, :, hs(h)]) * d for (b, h), d in zip(chains, decay)]
    o = [_bdot(q_ref[b, :, hs(h)] * e, s_scr[b, h]) + _bdot(a, vn)
         for (b, h), e, a, vn in zip(chains, e_c, attn, v_new)]
    for (b, h), g, vn in zip(chains, gc_c, v_new):
        g_last = g[c - 1:c, :]
        k_dec = k_ref[b, :, hs(h)] * jnp.exp(g_last - g)
        s_scr[b, h] = s_scr[b, h] * jnp.exp(g_last) + _bdot_tn(k_dec, vn)
    for (b, h), x in zip(chains, o):
        on = x * _rms_scale(x) * ng_ref[...]
        o_ref[b, :, hs(h)] = (on * gz_ref[b, :, hs(h)]).astype(BF16)

    sout_ref[...] = s_scr[...]


def _gdn(q, k, v, gz, gcol, grow, s0, ng, bb, c):
    B, L, _ = q.shape
    nc = L // c
    tile = lambda i, t: (i, t, 0)
    per_b4 = lambda i, t: (i, 0, 0, 0)
    in_specs = [
        pl.BlockSpec((bb, c, GDN_QK), tile),
        pl.BlockSpec((bb, c, GDN_QK), tile),
        pl.BlockSpec((bb, c, GDN_V), tile),
        pl.BlockSpec((bb, c, GDN_V), tile),
        pl.BlockSpec((bb, c, N_GATE), tile),
        pl.BlockSpec((bb, 1, N_GATE, c), lambda i, t: (i, t, 0, 0)),
        pl.BlockSpec((bb, GDN_HEADS, GDN_DK, GDN_DV), per_b4),
        pl.BlockSpec((1, GDN_DV), lambda i, t: (0, 0)),
    ]
    out_shape = (jax.ShapeDtypeStruct((B, L, GDN_V), BF16),
                 jax.ShapeDtypeStruct((B, GDN_HEADS, GDN_DK, GDN_DV), F32))
    out_specs = (pl.BlockSpec((bb, c, GDN_V), tile),
                 pl.BlockSpec((bb, GDN_HEADS, GDN_DK, GDN_DV), per_b4))
    return pl.pallas_call(
        functools.partial(_gdn_kernel, bb=bb, c=c),
        grid=(B // bb, nc), in_specs=in_specs, out_specs=out_specs, out_shape=out_shape,
        scratch_shapes=[pltpu.VMEM((bb, GDN_HEADS, GDN_DK, GDN_DV), F32)],
        compiler_params=_params(("arbitrary", "arbitrary")),
        name="gdn",
    )(q, k, v, gz, gcol, grow.reshape(B, N_GATE, nc, c).transpose(0, 2, 1, 3), s0, ng)


def _mixer_out_kernel(oa_ref, og_ref, x_ref, wout_ref, g2_ref, wr_ref, br_ref,
                      x1_ref, h2p_ref, ew_ref, ei_ref, cnt_ref, carry, *, tl):
    @pl.when(pl.program_id(0) == 0)
    def _():
        carry[...] = jnp.zeros_like(carry)

    mix = (jnp.dot(oa_ref[...], wout_ref[0:LRU_WIDTH, :], preferred_element_type=F32)
           + jnp.dot(og_ref[...], wout_ref[LRU_WIDTH:, :], preferred_element_type=F32))
    x1 = x_ref[...] + mix
    x1_ref[...] = x1
    h2 = x1 * _rms_scale(x1) * g2_ref[...]
    bits = pltpu.bitcast(h2.astype(BF16).astype(F32), jnp.uint32)
    h2p_ref[...] = (bits[:, :D_MODEL // 2] & jnp.uint32(0xFFFF0000)) | (bits[:, D_MODEL // 2:] >> 16)
    logits = _dot3(h2, wr_ref[...]) + br_ref[...]
    lane = lax.broadcasted_iota(jnp.int32, logits.shape, 1)
    lane_f = lane.astype(F32)
    neg = jnp.float32(-jnp.inf)
    big = jnp.float32(1e9)
    gmask = lane < N_GROUPS
    gl = jnp.where(gmask, logits, neg)
    gmax = jnp.max(gl, axis=1, keepdims=True)
    grp = jnp.min(jnp.where(gl == gmax, lane_f, big), axis=1, keepdims=True)
    p_grp = 1.0 / jnp.sum(jnp.where(gmask, jnp.exp(gl - gmax), 0.0), axis=1, keepdims=True)
    egrp = jnp.right_shift(lane - N_GROUPS, EXPERTS_PER_GROUP.bit_length() - 1).astype(F32)
    sel = (lane >= N_GROUPS) & (lane < N_GROUPS + N_EXPERTS) & (egrp == grp)
    el = jnp.where(sel, logits, neg)
    v1 = jnp.max(el, axis=1, keepdims=True)
    i1 = jnp.min(jnp.where(el == v1, lane_f, big), axis=1, keepdims=True)
    el2 = jnp.where(lane_f == i1, neg, el)
    v2 = jnp.max(el2, axis=1, keepdims=True)
    i2 = jnp.min(jnp.where(el2 == v2, lane_f, big), axis=1, keepdims=True)
    e2 = jnp.exp(v2 - v1)
    den = 1.0 / (1.0 + e2)
    w1 = den * p_grp
    w2 = (e2 * den) * p_grp
    ew_ref[...] = jnp.where(lane == 0, w1, jnp.where(lane == 1, w2, 0.0))

    ex1 = i1 - N_GROUPS
    ex2 = i2 - N_GROUPS
    hit1 = lane_f == ex1
    hit2 = lane_f == ex2
    oh = jnp.concatenate([jnp.where(hit1, 1.0, 0.0), jnp.where(hit2, 1.0, 0.0)], axis=1)
    rr = lax.broadcasted_iota(jnp.int32, (tl, tl), 0)
    cc = lax.broadcasted_iota(jnp.int32, (tl, tl), 1)
    before = jnp.dot(jnp.where(rr > cc, 1.0, 0.0).astype(BF16), oh.astype(BF16),
                     preferred_element_type=F32)
    tot = jnp.sum(oh, axis=0, keepdims=True)
    tot1, tot2 = tot[:, :LANES], tot[:, LANES:]
    c0 = carry[...]
    rank1 = jnp.sum(jnp.where(hit1, c0 + before[:, :LANES], 0.0), axis=1, keepdims=True)
    rank2 = jnp.sum(jnp.where(hit2, (c0 + tot1) + before[:, LANES:], 0.0), axis=1, keepdims=True)
    c1 = c0 + (tot1 + tot2)
    carry[...] = c1
    cnt_ref[...] = c1
    ei_ref[...] = jnp.where(lane == 0, ex1, jnp.where(lane == 1, ex2, jnp.where(
        lane == 2, rank1, jnp.where(lane == 3, rank2, 0.0)))).astype(jnp.int32)


def _mixer_out(oa, og, x, wts, tl):
    T = x.shape[0]
    tile = lambda i: (i, 0)
    const = lambda i: (0, 0)
    in_specs = [
        pl.BlockSpec((tl, LRU_WIDTH), tile),
        pl.BlockSpec((tl, GDN_V), tile),
        pl.BlockSpec((tl, D_MODEL), tile),
        pl.BlockSpec((D_MODEL, D_MODEL), const),
        pl.BlockSpec((1, D_MODEL), const),
        pl.BlockSpec((D_MODEL, LANES), const),
        pl.BlockSpec((1, LANES), const),
    ]
    out_shape = (jax.ShapeDtypeStruct((T, D_MODEL), F32),
                 jax.ShapeDtypeStruct((T, D_MODEL // 2), jnp.uint32),
                 jax.ShapeDtypeStruct((T, LANES), F32),
                 jax.ShapeDtypeStruct((T, LANES), jnp.int32),
                 jax.ShapeDtypeStruct((1, LANES), F32))
    out_specs = (pl.BlockSpec((tl, D_MODEL), tile), pl.BlockSpec((tl, D_MODEL // 2), tile),
                 pl.BlockSpec((tl, LANES), tile), pl.BlockSpec((tl, LANES), tile),
                 pl.BlockSpec((1, LANES), const))
    return pl.pallas_call(
        functools.partial(_mixer_out_kernel, tl=tl), grid=(T // tl,), in_specs=in_specs,
        out_specs=out_specs, out_shape=out_shape, scratch_shapes=[pltpu.VMEM((1, LANES), F32)],
        compiler_params=_params(("arbitrary",)), name="mixer_out",
    )(oa, og, x, wts["w_out"], wts["ln2_g"], wts["w_router"], wts["b_router"])


def _row_copy(src, src_row, dst, dst_row, sem):
    return pltpu.make_async_copy(src.at[pl.ds(src_row, 1), :], dst.at[pl.ds(dst_row, 1), :], sem)


DMA_UNROLL = 8


def _dispatch_kernel(pe_ref, dest_ref, h2p_hbm, xs_hbm, zbuf, sem_z, sem_r, *, td, rb, n_blocks):
    i = pl.program_id(0)

    def tail_copy(e):
        end = pe_ref[e]
        start = pl.multiple_of(end - rb, rb)
        return pltpu.make_async_copy(zbuf, xs_hbm.at[pl.ds(start, rb), :], sem_z)

    def nonempty(e):
        return pe_ref[e] > (pe_ref[e - 1] if e > 0 else 0)

    def unused_copy(j):
        return pltpu.make_async_copy(zbuf, xs_hbm.at[pl.ds(pl.multiple_of(j * rb, rb), rb), :], sem_z)

    @pl.when(i == 0)
    def _():
        zbuf[...] = jnp.zeros_like(zbuf)
        for e in range(N_EXPERTS):
            pl.when(nonempty(e))(lambda e=e: tail_copy(e).start())
        for e in range(N_EXPERTS):
            pl.when(nonempty(e))(lambda e=e: tail_copy(e).wait())
        first_unused = lax.shift_right_logical(pe_ref[N_EXPERTS - 1], rb.bit_length() - 1)
        lax.fori_loop(first_unused, n_blocks, lambda j, c: (unused_copy(j).start(), c)[1], 0)
        lax.fori_loop(first_unused, n_blocks, lambda j, c: (unused_copy(j).wait(), c)[1], 0)

    base = i * td

    def copies(r):
        return [_row_copy(h2p_hbm, base + r, xs_hbm, dest_ref[0, 0, k * td + r], sem_r)
                for k in range(TOP_K)]

    def start(r, carry):
        for cp in copies(r):
            cp.start()
        return carry

    def wait(r, carry):
        for cp in copies(r):
            cp.wait()
        return carry

    lax.fori_loop(0, td, start, 0, unroll=DMA_UNROLL)
    lax.fori_loop(0, td, wait, 0, unroll=DMA_UNROLL)


def _dispatch(pad_end, dest_tiles, h2p, n_rows, td, rb):
    T = h2p.shape[0]
    grid_spec = pltpu.PrefetchScalarGridSpec(
        num_scalar_prefetch=1, grid=(T // td,),
        in_specs=[pl.BlockSpec((1, 1, TOP_K * td), lambda i, pe: (i, 0, 0), memory_space=pltpu.SMEM),
                  pl.BlockSpec(memory_space=pl.ANY)],
        out_specs=pl.BlockSpec(memory_space=pl.ANY),
        scratch_shapes=[pltpu.VMEM((rb, D_MODEL // 2), jnp.uint32),
                        pltpu.SemaphoreType.DMA(()), pltpu.SemaphoreType.DMA(())],
    )
    return pl.pallas_call(
        functools.partial(_dispatch_kernel, td=td, rb=rb, n_blocks=n_rows // rb), grid_spec=grid_spec,
        out_shape=jax.ShapeDtypeStruct((n_rows, D_MODEL // 2), jnp.uint32),
        compiler_params=_params(("arbitrary",)), name="dispatch",
    )(pad_end, dest_tiles, h2p)


def _moe_kernel(be_ref, nu_ref, xs_ref, w1_ref, w3_ref, w2_ref, yb_ref):
    del be_ref
    half = D_MODEL // 2

    @pl.when(pl.program_id(0) >= nu_ref[0])
    def _():
        yb_ref[...] = jnp.zeros_like(yb_ref)

    @pl.when(pl.program_id(0) < nu_ref[0])
    def _():
        u = xs_ref[...]
        xa = pltpu.bitcast(u & jnp.uint32(0xFFFF0000), F32).astype(BF16)
        xb = pltpu.bitcast(u << 16, F32).astype(BF16)

        def proj(w_ref):
            return (jnp.dot(xa, w_ref[0, 0:half, :], preferred_element_type=F32)
                    + jnp.dot(xb, w_ref[0, half:, :], preferred_element_type=F32))

        hmid = _silu(proj(w1_ref)) * proj(w3_ref)
        yb_ref[...] = jnp.dot(hmid.astype(BF16), w2_ref[0], preferred_element_type=F32)


def _moe(block_e, n_used, xs, wts, rb):
    n_blocks = block_e.shape[0]

    grid_spec = pltpu.PrefetchScalarGridSpec(
        num_scalar_prefetch=2, grid=(n_blocks,),
        in_specs=[
            pl.BlockSpec((rb, D_MODEL // 2), lambda i, be, nu: (i, 0)),
            pl.BlockSpec((1, D_MODEL, D_EXPERT), lambda i, be, nu: (be[i], 0, 0)),
            pl.BlockSpec((1, D_MODEL, D_EXPERT), lambda i, be, nu: (be[i], 0, 0)),
            pl.BlockSpec((1, D_EXPERT, D_MODEL), lambda i, be, nu: (be[i], 0, 0)),
        ],
        out_specs=pl.BlockSpec((rb, D_MODEL), lambda i, be, nu: (i, 0)),
    )
    return pl.pallas_call(
        _moe_kernel, grid_spec=grid_spec,
        out_shape=jax.ShapeDtypeStruct((n_blocks * rb, D_MODEL), F32),
        compiler_params=_params(("arbitrary",)), name="moe",
    )(block_e, n_used, xs, wts["moe_w1"], wts["moe_w3"], wts["moe_w2"])


def _dispatch_plan(ei, cnt, rb, td, tc):
    T = ei.shape[0]
    n_blocks = -(-(T * TOP_K + N_EXPERTS * (rb - 1)) // rb)
    counts = cnt[0, :N_EXPERTS].astype(jnp.int32)
    padded = (counts + rb - 1) // rb * rb
    pad_end = jnp.cumsum(padded).astype(jnp.int32)
    pad_start = pad_end - padded
    e = ei[:, 0:TOP_K]
    rank = ei[:, TOP_K:2 * TOP_K]
    onehot = e[:, :, None] == jnp.arange(N_EXPERTS, dtype=jnp.int32)
    dest = rank + jnp.sum(jnp.where(onehot, pad_start, 0), axis=-1)

    def tiles(t):
        return dest.T.reshape(TOP_K, T // t, t).transpose(1, 0, 2).reshape(T // t, 1, TOP_K * t)

    blk_start = jnp.arange(n_blocks, dtype=jnp.int32) * rb
    block_e = jnp.minimum(jnp.sum(pad_end[None, :] <= blk_start[:, None], axis=1),
                          N_EXPERTS - 1).astype(jnp.int32)
    n_used = (pad_end[N_EXPERTS - 1:] // rb).astype(jnp.int32)
    return pad_end, tiles(td), tiles(tc), block_e, n_used, n_blocks * rb


def _combine_kernel(dest_ref, x1_ref, ew_ref, gf_ref, yb_hbm, y_ref, ybuf, sem, *, tc):
    def copies(r):
        return [pltpu.make_async_copy(yb_hbm.at[pl.ds(dest_ref[0, 0, k * tc + r], 1), :],
                                      ybuf.at[k, pl.ds(r, 1), :], sem) for k in range(TOP_K)]

    def start(r, carry):
        for cp in copies(r):
            cp.start()
        return carry

    def wait(r, carry):
        for cp in copies(r):
            cp.wait()
        return carry

    lax.fori_loop(0, tc, start, 0, unroll=DMA_UNROLL)
    lax.fori_loop(0, tc, wait, 0, unroll=DMA_UNROLL)
    w = ew_ref[...]
    moe = w[:, 0:1] * ybuf[0] + w[:, 1:2] * ybuf[1]
    x2 = x1_ref[...] + moe
    y_ref[...] = x2 * _rms_scale(x2) * gf_ref[...]


def _combine(dest_tiles, x1, ew, final_g, yb, tc):
    T = x1.shape[0]
    tile = lambda i: (i, 0)
    return pl.pallas_call(
        functools.partial(_combine_kernel, tc=tc), grid=(T // tc,),
        in_specs=[pl.BlockSpec((1, 1, TOP_K * tc), lambda i: (i, 0, 0), memory_space=pltpu.SMEM),
                  pl.BlockSpec((tc, D_MODEL), tile), pl.BlockSpec((tc, LANES), tile),
                  pl.BlockSpec((1, D_MODEL), lambda i: (0, 0)),
                  pl.BlockSpec(memory_space=pl.ANY)],
        out_specs=pl.BlockSpec((tc, D_MODEL), tile),
        out_shape=jax.ShapeDtypeStruct((T, D_MODEL), F32),
        scratch_shapes=[pltpu.VMEM((TOP_K, tc, D_MODEL), F32), pltpu.SemaphoreType.DMA(())],
        compiler_params=_params(("arbitrary",)), name="combine",
    )(dest_tiles, x1, ew, final_g, yb)


def _prep_weights(ln1_g, ln2_g, w_in, lru_conv_w, lru_conv_b, lru_wa, lru_ba, lru_wx, lru_bx,
                  lru_lambda, gdn_conv_w, gdn_a_log, gdn_dt_bias, gdn_norm_g, w_out,
                  router_group_w, router_group_b, router_expert_w, router_expert_b,
                  moe_w1, moe_w3, moe_w2, final_g):
    w_in0 = w_in[0]
    half = LRU_WIDTH // 2
    per_half = half // LRU_BLOCK

    def block_diag(w):
        w4 = w.reshape(2, per_half, LRU_BLOCK, LRU_BLOCK)
        eye = jnp.eye(per_half, dtype=w.dtype)
        return jnp.einsum("snij,nm->snimj", w4, eye).reshape(2, half, half)

    wgate = jnp.concatenate([block_diag(lru_wa[0]), block_diag(lru_wx[0])], axis=2).astype(BF16)
    zeros4 = jnp.zeros((GDN_HEADS,), F32)
    neg_a = jnp.concatenate([zeros4, jnp.exp(gdn_a_log[0].astype(F32))])
    dtb = jnp.concatenate([zeros4, gdn_dt_bias[0].astype(F32)])
    gate_par = jnp.stack([neg_a, dtb])
    w_router = jnp.zeros((D_MODEL, LANES), F32)
    w_router = w_router.at[:, :N_GROUPS].set(router_group_w[0])
    w_router = w_router.at[:, N_GROUPS:N_GROUPS + N_EXPERTS].set(router_expert_w[0])
    b_router = jnp.zeros((1, LANES), F32)
    b_router = b_router.at[0, :N_GROUPS].set(router_group_b[0])
    b_router = b_router.at[0, N_GROUPS:N_GROUPS + N_EXPERTS].set(router_expert_b[0])
    return {
        "ln1_g": ln1_g[0].reshape(1, D_MODEL), "ln2_g": ln2_g[0].reshape(1, D_MODEL),
        "w_main": w_in0[:, :N_MAIN].astype(BF16),
        "w_gate_col": w_in0[:, N_MAIN:].astype(BF16),
        "w_gate_row": w_in0[:, N_MAIN:].T.astype(BF16),
        "lru_cw": lru_conv_w[0], "lru_cb": lru_conv_b[0].reshape(1, LRU_WIDTH),
        "lru_wgate": wgate,
        "lru_ba": lru_ba[0].reshape(1, LRU_WIDTH), "lru_bx": lru_bx[0].reshape(1, LRU_WIDTH),
        "lru_lam": lru_lambda[0].reshape(1, LRU_WIDTH),
        "gdn_cw": gdn_conv_w[0], "gate_par_col": gate_par, "gate_par_row": gate_par.T,
        "gdn_ng": gdn_norm_g[0].reshape(1, GDN_DV),
        "w_out": w_out[0].astype(BF16),
        "w_router": w_router, "b_router": b_router,
        "moe_w1": moe_w1[0].astype(BF16), "moe_w3": moe_w3[0].astype(BF16),
        "moe_w2": moe_w2[0].astype(BF16),
        "final_g": final_g.reshape(1, D_MODEL),
    }


def _pad_conv_state(s):
    return jnp.pad(s, ((0, 0), (SUBLANES - (CONV_W - 1), 0), (0, 0)))


def _trunk(x, conv_l, h_l, conv_g, s_g, wts, tl_in, bb, tl_tok, rb, td):
    B, L, _ = x.shape
    T = B * L
    c = min(CHUNK, L)
    oa, q, k, v, gz, gcol, grow, nlc, nlh, ngc = _mixer_in(
        x, _pad_conv_state(conv_l), h_l.reshape(B, 1, LRU_WIDTH), _pad_conv_state(conv_g), wts, tl_in)
    og, ns = _gdn(q, k, v, gz, gcol, grow, s_g, wts["gdn_ng"], bb, c)
    x1, h2p, ew, ei, cnt = _mixer_out(oa.reshape(T, LRU_WIDTH), og.reshape(T, GDN_V),
                                      x.reshape(T, D_MODEL), wts, tl_tok)
    pad_end, dest_d, dest_c, block_e, n_used, n_rows = _dispatch_plan(ei, cnt, rb, td, tl_tok)
    xs = _dispatch(pad_end, dest_d, h2p, n_rows, td, rb)
    yb = _moe(block_e, n_used, xs, wts, rb)
    y = _combine(dest_c, x1, ew, wts["final_g"], yb, tl_tok)
    return (y.reshape(B, L, D_MODEL), nlc[None], nlh.reshape(1, B, LRU_WIDTH), ngc[None], ns[None])


def kernel(x_prompt, x_sample, state_lru_conv, state_lru_h, state_gdn_conv, state_gdn_S, ln1_g, ln2_g, w_in, lru_conv_w, lru_conv_b, lru_wa, lru_ba, lru_wx, lru_bx, lru_lambda, gdn_conv_w, gdn_a_log, gdn_dt_bias, gdn_norm_g, w_out, router_group_w, router_group_b, router_expert_w, router_expert_b, moe_w1, moe_w3, moe_w2, final_g):
    wts = _prep_weights(ln1_g, ln2_g, w_in, lru_conv_w, lru_conv_b, lru_wa, lru_ba, lru_wx, lru_bx,
                        lru_lambda, gdn_conv_w, gdn_a_log, gdn_dt_bias, gdn_norm_g, w_out,
                        router_group_w, router_group_b, router_expert_w, router_expert_b,
                        moe_w1, moe_w3, moe_w2, final_g)
    B, L, _ = x_prompt.shape
    dt = x_prompt.dtype
    y_p, p_lc, p_lh, p_gc, p_gs = _trunk(
        x_prompt,
        jnp.zeros((B, CONV_W - 1, LRU_WIDTH), dt), jnp.zeros((B, LRU_WIDTH), dt),
        jnp.zeros((B, CONV_W - 1, GDN_CONV_CH), dt), jnp.zeros((B, GDN_HEADS, GDN_DK, GDN_DV), dt),
        wts, tl_in=256, bb=4, tl_tok=512, rb=256, td=512)
    Bs, Ls, _ = x_sample.shape
    y_s, s_lc, s_lh, s_gc, s_gs = _trunk(
        x_sample, state_lru_conv[0], state_lru_h[0], state_gdn_conv[0], state_gdn_S[0],
        wts, tl_in=Ls, bb=2, tl_tok=Bs * Ls, rb=32, td=Bs * Ls)
    return (y_p, y_s, p_lc, p_lh, p_gc, p_gs, s_lc, s_lh, s_gc, s_gs)
```

```python
import functools

import jax
import jax.numpy as jnp
from jax import lax
from jax.experimental import pallas as pl
from jax.experimental.pallas import tpu as pltpu

F32 = jnp.float32
BF16 = jnp.bfloat16

D_MODEL = 1024
CONV_W = 4
LRU_WIDTH = 512
LRU_BLOCKS = 8
LRU_BLOCK = LRU_WIDTH // LRU_BLOCKS
LRU_C = 8.0
GDN_HEADS = 4
GDN_DK = 128
GDN_DV = 128
GDN_QK = GDN_HEADS * GDN_DK
GDN_V = GDN_HEADS * GDN_DV
GDN_CONV_CH = 2 * GDN_QK + GDN_V
N_GATE = 2 * GDN_HEADS
N_MAIN = 2 * LRU_WIDTH + GDN_CONV_CH + GDN_V
CHUNK = 64
N_GROUPS = 4
EXPERTS_PER_GROUP = 8
N_EXPERTS = N_GROUPS * EXPERTS_PER_GROUP
TOP_K = 2
D_EXPERT = 256
EPS = 1e-6

SUBLANES = 8
LANES = 128
MXU_DIM = 256
VMEM_LIMIT_BYTES = 48 * 1024 * 1024


def _bdot(a, b):
    return jnp.dot(a.astype(BF16), b.astype(BF16), preferred_element_type=F32)


def _bdot_nt(a, b):
    return lax.dot_general(a.astype(BF16), b.astype(BF16), (((1,), (1,)), ((), ())),
                           preferred_element_type=F32)


def _bdot_tn(a, b):
    return lax.dot_general(a.astype(BF16), b.astype(BF16), (((0,), (0,)), ((), ())),
                           preferred_element_type=F32)


def _split2(a):
    hi = a.astype(BF16)
    lo = (a - hi.astype(F32)).astype(BF16)
    return hi, lo


def _dot3(a, b):
    ah, al = _split2(a)
    bh, bl = _split2(b)
    d = functools.partial(jnp.dot, preferred_element_type=F32)
    return d(ah, bh) + (d(ah, bl) + d(al, bh))


def _dot_exact_rhs(a, b01):
    b = b01.astype(BF16)
    a1 = a.astype(BF16)
    r1 = a - a1.astype(F32)
    a2 = r1.astype(BF16)
    a3 = (r1 - a2.astype(F32)).astype(BF16)
    d = functools.partial(jnp.dot, preferred_element_type=F32)
    return d(a1, b) + (d(a2, b) + d(a3, b))


def _dot_exact_lhs(a01, b):
    a = a01.astype(BF16)
    b1 = b.astype(BF16)
    r1 = b - b1.astype(F32)
    b2 = r1.astype(BF16)
    b3 = (r1 - b2.astype(F32)).astype(BF16)
    d = functools.partial(jnp.dot, preferred_element_type=F32)
    return d(a, b1) + (d(a, b2) + d(a, b3))


def _sigmoid(x):
    return 1.0 / (1.0 + jnp.exp(-x))


def _softplus(x):
    return jnp.maximum(x, 0.0) + jnp.log(1.0 + jnp.exp(-jnp.abs(x)))


def _silu(x):
    return x * _sigmoid(x)


def _gelu_tanh(x):
    c = 0.7978845608028654
    return 0.5 * x * (1.0 + jnp.tanh(c * (x + 0.044715 * (x * x * x))))


def _rms_scale(x):
    return lax.rsqrt(jnp.mean(x * x, axis=-1, keepdims=True) + EPS)


def _params(sem):
    return pltpu.CompilerParams(dimension_semantics=sem, vmem_limit_bytes=VMEM_LIMIT_BYTES)


def _mixer_in_kernel(x_ref, g1_ref, wmain_ref, wgc_ref, wgr_ref,
                     lcw_ref, lcb_ref, wgate_ref, ba_ref, bx_ref, lam_ref,
                     gcw_ref, gpc_ref, gpr_ref,
                     lconv0_ref, lh0_ref, gconv0_ref,
                     outa_ref, q_ref, k_ref, v_ref, gz_ref, gcol_ref, grow_ref,
                     nlc_ref, nlh_ref, ngc_ref,
                     xl_buf, xg_buf, hcar, acum, bcum, *, tl):
    lt = pl.program_id(1)
    s1, s2, s3 = LRU_WIDTH, 2 * LRU_WIDTH, 2 * LRU_WIDTH + GDN_CONV_CH

    @pl.when(lt == 0)
    def _():
        xl_buf[0:SUBLANES, :] = lconv0_ref[0]
        xg_buf[0:SUBLANES, :] = gconv0_ref[0]
        hcar[...] = lh0_ref[0]

    @pl.when(lt > 0)
    def _():
        xl_buf[0:SUBLANES, :] = xl_buf[tl:tl + SUBLANES, :]
        xg_buf[0:SUBLANES, :] = xg_buf[tl:tl + SUBLANES, :]

    x = x_ref[0]
    hn = (x * _rms_scale(x) * g1_ref[...]).astype(BF16)

    xl_buf[SUBLANES:SUBLANES + tl, :] = jnp.dot(hn, wmain_ref[:, 0:s1], preferred_element_type=F32)
    base = SUBLANES - (CONV_W - 1)

    def causal_conv(buf, w_ref):
        xall = buf[...]
        acc = None
        for j in range(CONV_W):
            back = CONV_W - 1 - j
            shifted = pltpu.roll(xall, back, axis=0) if back else xall
            term = shifted[SUBLANES:, :] * w_ref[j:j + 1, :]
            acc = term if acc is None else acc + term
        return acc

    xc = causal_conv(xl_buf, lcw_ref) + lcb_ref[...]
    nlc_ref[0] = xl_buf[tl + base:tl + SUBLANES, :]

    half = LRU_WIDTH // 2
    pre = [_bdot(xc[:, s * half:(s + 1) * half], wgate_ref[s]) for s in range(2)]
    r_pre = jnp.concatenate([pre[0][:, :half], pre[1][:, :half]], axis=1)
    i_pre = jnp.concatenate([pre[0][:, half:], pre[1][:, half:]], axis=1)
    r = _sigmoid(r_pre + ba_ref[...])
    ig = _sigmoid(i_pre + bx_ref[...])
    log_a = (-LRU_C) * r * _softplus(-lam_ref[...])
    a = jnp.exp(log_a)
    b = jnp.sqrt(1.0 - jnp.exp(2.0 * log_a)) * (ig * xc)

    ng = tl // SUBLANES
    a3 = a.reshape(ng, SUBLANES, LRU_WIDTH)
    b3 = b.reshape(ng, SUBLANES, LRU_WIDTH)
    row = lax.broadcasted_iota(jnp.int32, (ng, SUBLANES, LRU_WIDTH), 1)
    sh = 1
    while sh < SUBLANES:
        a_s = pltpu.roll(a3, sh, axis=1)
        b_s = pltpu.roll(b3, sh, axis=1)
        m = row >= sh
        b3 = jnp.where(m, a3 * b_s + b3, b3)
        a3 = jnp.where(m, a3 * a_s, a3)
        sh *= 2
    h0 = hcar[...]
    if ng % SUBLANES:
        parts = []
        hlast = h0
        for g in range(ng):
            hg = a3[g] * hlast + b3[g]
            parts.append(hg)
            hlast = hg[SUBLANES - 1:SUBLANES, :]
        hs3 = jnp.stack(parts)
    else:
        def group_last(scr, val3):
            val = val3.reshape(tl, LRU_WIDTH)
            nch = LRU_WIDTH // LANES
            for ch in range(nch):
                scr[ch] = val[:, ch * LANES:(ch + 1) * LANES]
            return jnp.concatenate(
                [scr[ch, pl.ds(SUBLANES - 1, ng, stride=SUBLANES), :] for ch in range(nch)], axis=1)

        ag = group_last(acum, a3)
        bg = group_last(bcum, b3)
        grow_i = lax.broadcasted_iota(jnp.int32, (ng, LRU_WIDTH), 0)
        sh = 1
        while sh < ng:
            a_s = pltpu.roll(ag, sh, axis=0)
            b_s = pltpu.roll(bg, sh, axis=0)
            m = grow_i >= sh
            bg = jnp.where(m, ag * b_s + bg, bg)
            ag = jnp.where(m, ag * a_s, ag)
            sh *= 2
        hend = ag * h0 + bg
        hin = jnp.where(grow_i == 0, h0, pltpu.roll(hend, 1, axis=0))
        hs3 = a3 * hin.reshape(ng, 1, LRU_WIDTH) + b3
        hlast = hend[ng - 1:ng, :]
    hcar[...] = hlast
    nlh_ref[0] = hlast
    y = jnp.dot(hn, wmain_ref[:, s1:s2], preferred_element_type=F32)
    outa_ref[0] = (hs3.reshape(tl, LRU_WIDTH) * _gelu_tanh(y)).astype(BF16)

    xg_buf[SUBLANES:SUBLANES + tl, :] = jnp.dot(hn, wmain_ref[:, s2:s3], preferred_element_type=F32)
    ngc_ref[0] = xg_buf[tl + base:tl + SUBLANES, :]
    qkv = _silu(causal_conv(xg_buf, gcw_ref))
    for h in range(GDN_HEADS):
        qh = qkv[:, h * GDN_DK:(h + 1) * GDN_DK]
        kh = qkv[:, GDN_QK + h * GDN_DK:GDN_QK + (h + 1) * GDN_DK]
        qs = lax.rsqrt(jnp.sum(qh * qh, axis=-1, keepdims=True) + EPS) * (GDN_DK ** -0.5)
        ks = lax.rsqrt(jnp.sum(kh * kh, axis=-1, keepdims=True) + EPS)
        q_ref[0, :, h * GDN_DK:(h + 1) * GDN_DK] = qh * qs
        k_ref[0, :, h * GDN_DK:(h + 1) * GDN_DK] = kh * ks
    v_ref[0] = qkv[:, 2 * GDN_QK:]
    gz_ref[0] = _silu(jnp.dot(hn, wmain_ref[:, s3:], preferred_element_type=F32))

    gc = jnp.dot(hn, wgc_ref[...], preferred_element_type=F32)
    lane = lax.broadcasted_iota(jnp.int32, gc.shape, 1)
    gcol_ref[0] = jnp.where(lane < GDN_HEADS, _sigmoid(gc),
                            -gpc_ref[0:1, :] * _softplus(gc + gpc_ref[1:2, :]))
    gr = lax.dot_general(wgr_ref[...], hn, (((1,), (1,)), ((), ())), preferred_element_type=F32)
    sub = lax.broadcasted_iota(jnp.int32, gr.shape, 0)
    grow_ref[0] = jnp.where(sub < GDN_HEADS, _sigmoid(gr),
                            -gpr_ref[:, 0:1] * _softplus(gr + gpr_ref[:, 1:2]))


def _mixer_in(x, conv_l8, h_l, conv_g8, wts, tl):
    B, L, _ = x.shape
    nl = L // tl
    const2 = lambda b, l: (0, 0)
    const3 = lambda b, l: (0, 0, 0)
    per_b3 = lambda b, l: (b, 0, 0)
    tile3 = lambda b, l: (b, l, 0)
    in_specs = [
        pl.BlockSpec((1, tl, D_MODEL), tile3),
        pl.BlockSpec((1, D_MODEL), const2),
        pl.BlockSpec((D_MODEL, N_MAIN), const2),
        pl.BlockSpec((D_MODEL, N_GATE), const2),
        pl.BlockSpec((N_GATE, D_MODEL), const2),
        pl.BlockSpec((CONV_W, LRU_WIDTH), const2),
        pl.BlockSpec((1, LRU_WIDTH), const2),
        pl.BlockSpec((2, LRU_WIDTH // 2, LRU_WIDTH), const3),
        pl.BlockSpec((1, LRU_WIDTH), const2),
        pl.BlockSpec((1, LRU_WIDTH), const2),
        pl.BlockSpec((1, LRU_WIDTH), const2),
        pl.BlockSpec((CONV_W, GDN_CONV_CH), const2),
        pl.BlockSpec((2, N_GATE), const2),
        pl.BlockSpec((N_GATE, 2), const2),
        pl.BlockSpec((1, SUBLANES, LRU_WIDTH), per_b3),
        pl.BlockSpec((1, 1, LRU_WIDTH), per_b3),
        pl.BlockSpec((1, SUBLANES, GDN_CONV_CH), per_b3),
    ]
    out_shape = (
        jax.ShapeDtypeStruct((B, L, LRU_WIDTH), BF16),
        jax.ShapeDtypeStruct((B, L, GDN_QK), F32),
        jax.ShapeDtypeStruct((B, L, GDN_QK), F32),
        jax.ShapeDtypeStruct((B, L, GDN_V), F32),
        jax.ShapeDtypeStruct((B, L, GDN_V), F32),
        jax.ShapeDtypeStruct((B, L, N_GATE), F32),
        jax.ShapeDtypeStruct((B, N_GATE, L), F32),
        jax.ShapeDtypeStruct((B, CONV_W - 1, LRU_WIDTH), F32),
        jax.ShapeDtypeStruct((B, 1, LRU_WIDTH), F32),
        jax.ShapeDtypeStruct((B, CONV_W - 1, GDN_CONV_CH), F32),
    )
    out_specs = (
        pl.BlockSpec((1, tl, LRU_WIDTH), tile3),
        pl.BlockSpec((1, tl, GDN_QK), tile3),
        pl.BlockSpec((1, tl, GDN_QK), tile3),
        pl.BlockSpec((1, tl, GDN_V), tile3),
        pl.BlockSpec((1, tl, GDN_V), tile3),
        pl.BlockSpec((1, tl, N_GATE), tile3),
        pl.BlockSpec((1, N_GATE, tl), lambda b, l: (b, 0, l)),
        pl.BlockSpec((1, CONV_W - 1, LRU_WIDTH), per_b3),
        pl.BlockSpec((1, 1, LRU_WIDTH), per_b3),
        pl.BlockSpec((1, CONV_W - 1, GDN_CONV_CH), per_b3),
    )
    scratch = [
        pltpu.VMEM((tl + SUBLANES, LRU_WIDTH), F32),
        pltpu.VMEM((tl + SUBLANES, GDN_CONV_CH), F32),
        pltpu.VMEM((1, LRU_WIDTH), F32),
        pltpu.VMEM((LRU_WIDTH // LANES, tl, LANES), F32),
        pltpu.VMEM((LRU_WIDTH // LANES, tl, LANES), F32),
    ]
    return pl.pallas_call(
        functools.partial(_mixer_in_kernel, tl=tl),
        grid=(B, nl), in_specs=in_specs, out_specs=out_specs, out_shape=out_shape,
        scratch_shapes=scratch, compiler_params=_params(("arbitrary", "arbitrary")),
        name="mixer_in",
    )(x, wts["ln1_g"], wts["w_main"], wts["w_gate_col"], wts["w_gate_row"],
      wts["lru_cw"], wts["lru_cb"], wts["lru_wgate"], wts["lru_ba"], wts["lru_bx"], wts["lru_lam"],
      wts["gdn_cw"], wts["gate_par_col"], wts["gate_par_row"],
      conv_l8, h_l, conv_g8)


def _tri_inverse_all(lms, c):
    r = lax.broadcasted_iota(jnp.int32, (c, c), 0)
    col = lax.broadcasted_iota(jnp.int32, (c, c), 1)
    eye = (r == col).astype(F32)

    def same_block(n):
        sh = n.bit_length() - 1
        return jnp.right_shift(r, sh) == jnp.right_shift(col, sh)

    blk8 = same_block(SUBLANES)
    l8 = [jnp.where(blk8, lm, 0.0) for lm in lms]
    l2 = [_dot3(a, a) for a in l8]
    p = [(eye - a) + _dot3(eye - a, b) for a, b in zip(l8, l2)]
    l4 = [_dot3(a, a) for a in l2]
    p = [a + _dot3(a, b) for a, b in zip(p, l4)]
    n = SUBLANES
    while n < c:
        off = same_block(2 * n) & jnp.logical_not(same_block(n))
        lo = [jnp.where(off, lm, 0.0) for lm in lms]
        t = [_dot3(a, b) for a, b in zip(p, lo)]
        p = [a - _dot3(b, a) for a, b in zip(p, t)]
        n *= 2
    return p


def _gdn_kernel(q_ref, k_ref, v_ref, gz_ref, gcol_ref, grow_ref, s0_ref, ng_ref,
                o_ref, sout_ref, s_scr, *, bb, c):
    ct = pl.program_id(1)

    @pl.when(ct == 0)
    def _():
        s_scr[...] = s0_ref[...]

    r = lax.broadcasted_iota(jnp.int32, (c, c), 0)
    col = lax.broadcasted_iota(jnp.int32, (c, c), 1)
    lower = r >= col
    strict = r > col
    tri_l = lower.astype(F32)
    tri_u = (r <= col).astype(F32)

    chains = [(b, h) for b in range(bb) for h in range(GDN_HEADS)]

    def hs(h):
        return slice(h * GDN_DK, (h + 1) * GDN_DK)

    gates_c = [gcol_ref[b] for b in range(bb)]
    gcum_c = [_dot_exact_lhs(tri_l, g) for g in gates_c]
    gcum_r = [_dot_exact_rhs(grow_ref[b, 0], tri_u) for b in range(bb)]
    beta = [gates_c[b][:, h:h + 1] for b, h in chains]
    gc_c = [gcum_c[b][:, GDN_HEADS + h:GDN_HEADS + h + 1] for b, h in chains]
    gc_r = [gcum_r[b][GDN_HEADS + h:GDN_HEADS + h + 1, :] for b, h in chains]
    decay = [jnp.where(lower, jnp.exp(jnp.where(lower, a - b, 0.0)), 0.0) for a, b in zip(gc_c, gc_r)]
    e_c = [jnp.exp(a) for a in gc_c]
    kb = [k_ref[b, :, hs(h)] * bt for (b, h), bt in zip(chains, beta)]
    lmat = [jnp.where(strict, _bdot_nt(a, k_ref[b, :, hs(h)]) * d, 0.0)
            for (b, h), a, d in zip(chains, kb, decay)]
    t_inv = _tri_inverse_all(lmat, c)
    uw = [_bdot(t, jnp.concatenate([v_ref[b, :, hs(h)] * bt, a * e], axis=1))
          for (b, h), t, bt, a, e in zip(chains, t_inv, beta, kb, e_c)]
    v_new = [x[:, :GDN_DV] - _bdot(x[:, GDN_DV:], s_scr[b, h]) for (b, h), x in zip(chains, uw)]
    attn = [_bdot_nt(q_ref[b, :, hs(h)], k_ref[b, :, hs(h)]) * d for (b, h), d in zip(chains, decay)]
    o = [_bdot(q_ref[b, :, hs(h)] * e, s_scr[b, h]) + _bdot(a, vn)
         for (b, h), e, a, vn in zip(chains, e_c, attn, v_new)]
    for (b, h), g, vn in zip(chains, gc_c, v_new):
        g_last = g[c - 1:c, :]
        k_dec = k_ref[b, :, hs(h)] * jnp.exp(g_last - g)
        s_scr[b, h] = s_scr[b, h] * jnp.exp(g_last) + _bdot_tn(k_dec, vn)
    for (b, h), x in zip(chains, o):
        on = x * _rms_scale(x) * ng_ref[...]
        o_ref[b, :, hs(h)] = (on * gz_ref[b, :, hs(h)]).astype(BF16)

    sout_ref[...] = s_scr[...]


def _gdn(q, k, v, gz, gcol, grow, s0, ng, bb, c):
    B, L, _ = q.shape
    nc = L // c
    tile = lambda i, t: (i, t, 0)
    per_b4 = lambda i, t: (i, 0, 0, 0)
    in_specs = [
        pl.BlockSpec((bb, c, GDN_QK), tile),
        pl.BlockSpec((bb, c, GDN_QK), tile),
        pl.BlockSpec((bb, c, GDN_V), tile),
        pl.BlockSpec((bb, c, GDN_V), tile),
        pl.BlockSpec((bb, c, N_GATE), tile),
        pl.BlockSpec((bb, 1, N_GATE, c), lambda i, t: (i, t, 0, 0)),
        pl.BlockSpec((bb, GDN_HEADS, GDN_DK, GDN_DV), per_b4),
        pl.BlockSpec((1, GDN_DV), lambda i, t: (0, 0)),
    ]
    out_shape = (jax.ShapeDtypeStruct((B, L, GDN_V), BF16),
                 jax.ShapeDtypeStruct((B, GDN_HEADS, GDN_DK, GDN_DV), F32))
    out_specs = (pl.BlockSpec((bb, c, GDN_V), tile),
                 pl.BlockSpec((bb, GDN_HEADS, GDN_DK, GDN_DV), per_b4))
    return pl.pallas_call(
        functools.partial(_gdn_kernel, bb=bb, c=c),
        grid=(B // bb, nc), in_specs=in_specs, out_specs=out_specs, out_shape=out_shape,
        scratch_shapes=[pltpu.VMEM((bb, GDN_HEADS, GDN_DK, GDN_DV), F32)],
        compiler_params=_params(("arbitrary", "arbitrary")),
        name="gdn",
    )(q, k, v, gz, gcol, grow.reshape(B, N_GATE, nc, c).transpose(0, 2, 1, 3), s0, ng)


def _mixer_out_kernel(oa_ref, og_ref, x_ref, wout_ref, g2_ref, wr_ref, br_ref,
                      x1_ref, h2p_ref, ew_ref, ei_ref, cnt_ref, carry, *, tl):
    @pl.when(pl.program_id(0) == 0)
    def _():
        carry[...] = jnp.zeros_like(carry)

    mix = (jnp.dot(oa_ref[...], wout_ref[0:LRU_WIDTH, :], preferred_element_type=F32)
           + jnp.dot(og_ref[...], wout_ref[LRU_WIDTH:, :], preferred_element_type=F32))
    x1 = x_ref[...] + mix
    x1_ref[...] = x1
    h2 = x1 * _rms_scale(x1) * g2_ref[...]
    bits = pltpu.bitcast(h2.astype(BF16).astype(F32), jnp.uint32)
    h2p_ref[...] = (bits[:, :D_MODEL // 2] & jnp.uint32(0xFFFF0000)) | (bits[:, D_MODEL // 2:] >> 16)
    logits = _dot3(h2, wr_ref[...]) + br_ref[...]
    lane = lax.broadcasted_iota(jnp.int32, logits.shape, 1)
    lane_f = lane.astype(F32)
    neg = jnp.float32(-jnp.inf)
    big = jnp.float32(1e9)
    gmask = lane < N_GROUPS
    gl = jnp.where(gmask, logits, neg)
    gmax = jnp.max(gl, axis=1, keepdims=True)
    grp = jnp.min(jnp.where(gl == gmax, lane_f, big), axis=1, keepdims=True)
    p_grp = 1.0 / jnp.sum(jnp.where(gmask, jnp.exp(gl - gmax), 0.0), axis=1, keepdims=True)
    egrp = jnp.right_shift(lane - N_GROUPS, EXPERTS_PER_GROUP.bit_length() - 1).astype(F32)
    sel = (lane >= N_GROUPS) & (lane < N_GROUPS + N_EXPERTS) & (egrp == grp)
    el = jnp.where(sel, logits, neg)
    v1 = jnp.max(el, axis=1, keepdims=True)
    i1 = jnp.min(jnp.where(el == v1, lane_f, big), axis=1, keepdims=True)
    el2 = jnp.where(lane_f == i1, neg, el)
    v2 = jnp.max(el2, axis=1, keepdims=True)
    i2 = jnp.min(jnp.where(el2 == v2, lane_f, big), axis=1, keepdims=True)
    e2 = jnp.exp(v2 - v1)
    den = 1.0 / (1.0 + e2)
    w1 = den * p_grp
    w2 = (e2 * den) * p_grp
    ew_ref[...] = jnp.where(lane == 0, w1, jnp.where(lane == 1, w2, 0.0))

    ex1 = i1 - N_GROUPS
    ex2 = i2 - N_GROUPS
    hit1 = lane_f == ex1
    hit2 = lane_f == ex2
    oh = jnp.concatenate([jnp.where(hit1, 1.0, 0.0), jnp.where(hit2, 1.0, 0.0)], axis=1)
    rr = lax.broadcasted_iota(jnp.int32, (tl, tl), 0)
    cc = lax.broadcasted_iota(jnp.int32, (tl, tl), 1)
    before = jnp.dot(jnp.where(rr > cc, 1.0, 0.0).astype(BF16), oh.astype(BF16),
                     preferred_element_type=F32)
    tot = jnp.sum(oh, axis=0, keepdims=True)
    tot1, tot2 = tot[:, :LANES], tot[:, LANES:]
    c0 = carry[...]
    rank1 = jnp.sum(jnp.where(hit1, c0 + before[:, :LANES], 0.0), axis=1, keepdims=True)
    rank2 = jnp.sum(jnp.where(hit2, (c0 + tot1) + before[:, LANES:], 0.0), axis=1, keepdims=True)
    c1 = c0 + (tot1 + tot2)
    carry[...] = c1
    cnt_ref[...] = c1
    ei_ref[...] = jnp.where(lane == 0, ex1, jnp.where(lane == 1, ex2, jnp.where(
        lane == 2, rank1, jnp.where(lane == 3, rank2, 0.0)))).astype(jnp.int32)


def _mixer_out(oa, og, x, wts, tl):
    T = x.shape[0]
    tile = lambda i: (i, 0)
    const = lambda i: (0, 0)
    in_specs = [
        pl.BlockSpec((tl, LRU_WIDTH), tile),
        pl.BlockSpec((tl, GDN_V), tile),
        pl.BlockSpec((tl, D_MODEL), tile),
        pl.BlockSpec((D_MODEL, D_MODEL), const),
        pl.BlockSpec((1, D_MODEL), const),
        pl.BlockSpec((D_MODEL, LANES), const),
        pl.BlockSpec((1, LANES), const),
    ]
    out_shape = (jax.ShapeDtypeStruct((T, D_MODEL), F32),
                 jax.ShapeDtypeStruct((T, D_MODEL // 2), jnp.uint32),
                 jax.ShapeDtypeStruct((T, LANES), F32),
                 jax.ShapeDtypeStruct((T, LANES), jnp.int32),
                 jax.ShapeDtypeStruct((1, LANES), F32))
    out_specs = (pl.BlockSpec((tl, D_MODEL), tile), pl.BlockSpec((tl, D_MODEL // 2), tile),
                 pl.BlockSpec((tl, LANES), tile), pl.BlockSpec((tl, LANES), tile),
                 pl.BlockSpec((1, LANES), const))
    return pl.pallas_call(
        functools.partial(_mixer_out_kernel, tl=tl), grid=(T // tl,), in_specs=in_specs,
        out_specs=out_specs, out_shape=out_shape, scratch_shapes=[pltpu.VMEM((1, LANES), F32)],
        compiler_params=_params(("arbitrary",)), name="mixer_out",
    )(oa, og, x, wts["w_out"], wts["ln2_g"], wts["w_router"], wts["b_router"])


def _row_copy(src, src_row, dst, dst_row, sem):
    return pltpu.make_async_copy(src.at[pl.ds(src_row, 1), :], dst.at[pl.ds(dst_row, 1), :], sem)


DMA_UNROLL = 8


def _dispatch_kernel(pe_ref, dest_ref, h2p_ref, xs_hbm, zbuf, sem_z, sem_r, *, td, rb, n_blocks):
    i = pl.program_id(0)

    def tail_copy(e):
        end = pe_ref[e]
        start = pl.multiple_of(end - rb, rb)
        return pltpu.make_async_copy(zbuf, xs_hbm.at[pl.ds(start, rb), :], sem_z)

    def nonempty(e):
        return pe_ref[e] > (pe_ref[e - 1] if e > 0 else 0)

    def unused_copy(j):
        return pltpu.make_async_copy(zbuf, xs_hbm.at[pl.ds(pl.multiple_of(j * rb, rb), rb), :], sem_z)

    @pl.when(i == 0)
    def _():
        zbuf[...] = jnp.zeros_like(zbuf)
        for e in range(N_EXPERTS):
            pl.when(nonempty(e))(lambda e=e: tail_copy(e).start())
        for e in range(N_EXPERTS):
            pl.when(nonempty(e))(lambda e=e: tail_copy(e).wait())
        first_unused = lax.shift_right_logical(pe_ref[N_EXPERTS - 1], rb.bit_length() - 1)
        lax.fori_loop(first_unused, n_blocks, lambda j, c: (unused_copy(j).start(), c)[1], 0)
        lax.fori_loop(first_unused, n_blocks, lambda j, c: (unused_copy(j).wait(), c)[1], 0)

    def copies(r):
        return [_row_copy(h2p_ref, r, xs_hbm, dest_ref[0, 0, k * td + r], sem_r)
                for k in range(TOP_K)]

    def start(r, carry):
        for k, cp in enumerate(copies(r)):
            cp.start(priority=k)
        return carry

    def wait(r, carry):
        for cp in copies(r):
            cp.wait()
        return carry

    lax.fori_loop(0, td, start, 0, unroll=DMA_UNROLL)
    lax.fori_loop(0, td, wait, 0, unroll=DMA_UNROLL)


def _dispatch(pad_end, dest_tiles, h2p, n_rows, td, rb):
    T = h2p.shape[0]
    grid_spec = pltpu.PrefetchScalarGridSpec(
        num_scalar_prefetch=1, grid=(T // td,),
        in_specs=[pl.BlockSpec((1, 1, TOP_K * td), lambda i, pe: (i, 0, 0), memory_space=pltpu.SMEM),
                  pl.BlockSpec((td, D_MODEL // 2), lambda i, pe: (i, 0))],
        out_specs=pl.BlockSpec(memory_space=pl.ANY),
        scratch_shapes=[pltpu.VMEM((rb, D_MODEL // 2), jnp.uint32),
                        pltpu.SemaphoreType.DMA(()), pltpu.SemaphoreType.DMA(())],
    )
    return pl.pallas_call(
        functools.partial(_dispatch_kernel, td=td, rb=rb, n_blocks=n_rows // rb), grid_spec=grid_spec,
        out_shape=jax.ShapeDtypeStruct((n_rows, D_MODEL // 2), jnp.uint32),
        compiler_params=_params(("arbitrary",)), name="dispatch",
    )(pad_end, dest_tiles, h2p)


def _moe_kernel(be_ref, nu_ref, xs_ref, w1_ref, w3_ref, w2_ref, yb_ref):
    del be_ref
    half = D_MODEL // 2

    @pl.when(pl.program_id(0) >= nu_ref[0])
    def _():
        yb_ref[...] = jnp.zeros_like(yb_ref)

    @pl.when(pl.program_id(0) < nu_ref[0])
    def _():
        u = xs_ref[...]
        xa = pltpu.bitcast(u & jnp.uint32(0xFFFF0000), F32).astype(BF16)
        xb = pltpu.bitcast(u << 16, F32).astype(BF16)

        def proj(w_ref):
            return (jnp.dot(xa, w_ref[0, 0:half, :], preferred_element_type=F32)
                    + jnp.dot(xb, w_ref[0, half:, :], preferred_element_type=F32))

        hmid = _silu(proj(w1_ref)) * proj(w3_ref)
        yb_ref[...] = jnp.dot(hmid.astype(BF16), w2_ref[0], preferred_element_type=F32)


def _moe(block_e, n_used, xs, wts, rb):
    n_blocks = block_e.shape[0]

    grid_spec = pltpu.PrefetchScalarGridSpec(
        num_scalar_prefetch=2, grid=(n_blocks,),
        in_specs=[
            pl.BlockSpec((rb, D_MODEL // 2), lambda i, be, nu: (i, 0)),
            pl.BlockSpec((1, D_MODEL, D_EXPERT), lambda i, be, nu: (be[i], 0, 0)),
            pl.BlockSpec((1, D_MODEL, D_EXPERT), lambda i, be, nu: (be[i], 0, 0)),
            pl.BlockSpec((1, D_EXPERT, D_MODEL), lambda i, be, nu: (be[i], 0, 0)),
        ],
        out_specs=pl.BlockSpec((rb, D_MODEL), lambda i, be, nu: (i, 0)),
    )
    return pl.pallas_call(
        _moe_kernel, grid_spec=grid_spec,
        out_shape=jax.ShapeDtypeStruct((n_blocks * rb, D_MODEL), F32),
        compiler_params=_params(("arbitrary",)), name="moe",
    )(block_e, n_used, xs, wts["moe_w1"], wts["moe_w3"], wts["moe_w2"])


def _dispatch_plan(ei, cnt, rb, td, tc):
    T = ei.shape[0]
    n_blocks = -(-(T * TOP_K + N_EXPERTS * (rb - 1)) // rb)
    counts = cnt[0, :N_EXPERTS].astype(jnp.int32)
    padded = (counts + rb - 1) // rb * rb
    pad_end = jnp.cumsum(padded).astype(jnp.int32)
    pad_start = pad_end - padded
    e = ei[:, 0:TOP_K]
    rank = ei[:, TOP_K:2 * TOP_K]
    onehot = e[:, :, None] == jnp.arange(N_EXPERTS, dtype=jnp.int32)
    dest = rank + jnp.sum(jnp.where(onehot, pad_start, 0), axis=-1)

    def tiles(t):
        return dest.T.reshape(TOP_K, T // t, t).transpose(1, 0, 2).reshape(T // t, 1, TOP_K * t)

    blk_start = jnp.arange(n_blocks, dtype=jnp.int32) * rb
    block_e = jnp.minimum(jnp.sum(pad_end[None, :] <= blk_start[:, None], axis=1),
                          N_EXPERTS - 1).astype(jnp.int32)
    n_used = (pad_end[N_EXPERTS - 1:] // rb).astype(jnp.int32)
    return pad_end, tiles(td), tiles(tc), block_e, n_used, n_blocks * rb


def _combine_kernel(dest_ref, x1_ref, ew_ref, gf_ref, yb_hbm, y_ref, ybuf, sem, *, tc):
    def copies(r):
        return [pltpu.make_async_copy(yb_hbm.at[pl.ds(dest_ref[0, 0, k * tc + r], 1), :],
                                      ybuf.at[k, pl.ds(r, 1), :], sem) for k in range(TOP_K)]

    def start(r, carry):
        for k, cp in enumerate(copies(r)):
            cp.start(priority=k)
        return carry

    def wait(r, carry):
        for cp in copies(r):
            cp.wait()
        return carry

    lax.fori_loop(0, tc, start, 0, unroll=DMA_UNROLL)
    lax.fori_loop(0, tc, wait, 0, unroll=DMA_UNROLL)
    w = ew_ref[...]
    moe = w[:, 0:1] * ybuf[0] + w[:, 1:2] * ybuf[1]
    x2 = x1_ref[...] + moe
    y_ref[...] = x2 * _rms_scale(x2) * gf_ref[...]


def _combine(dest_tiles, x1, ew, final_g, yb, tc):
    T = x1.shape[0]
    tile = lambda i: (i, 0)
    return pl.pallas_call(
        functools.partial(_combine_kernel, tc=tc), grid=(T // tc,),
        in_specs=[pl.BlockSpec((1, 1, TOP_K * tc), lambda i: (i, 0, 0), memory_space=pltpu.SMEM),
                  pl.BlockSpec((tc, D_MODEL), tile), pl.BlockSpec((tc, LANES), tile),
                  pl.BlockSpec((1, D_MODEL), lambda i: (0, 0)),
                  pl.BlockSpec(memory_space=pl.ANY)],
        out_specs=pl.BlockSpec((tc, D_MODEL), tile),
        out_shape=jax.ShapeDtypeStruct((T, D_MODEL), F32),
        scratch_shapes=[pltpu.VMEM((TOP_K, tc, D_MODEL), F32), pltpu.SemaphoreType.DMA(())],
        compiler_params=_params(("arbitrary",)), name="combine",
    )(dest_tiles, x1, ew, final_g, yb)


def _prep_weights(ln1_g, ln2_g, w_in, lru_conv_w, lru_conv_b, lru_wa, lru_ba, lru_wx, lru_bx,
                  lru_lambda, gdn_conv_w, gdn_a_log, gdn_dt_bias, gdn_norm_g, w_out,
                  router_group_w, router_group_b, router_expert_w, router_expert_b,
                  moe_w1, moe_w3, moe_w2, final_g):
    w_in0 = w_in[0]
    half = LRU_WIDTH // 2
    per_half = half // LRU_BLOCK

    def block_diag(w):
        w4 = w.reshape(2, per_half, LRU_BLOCK, LRU_BLOCK)
        eye = jnp.eye(per_half, dtype=w.dtype)
        return jnp.einsum("snij,nm->snimj", w4, eye).reshape(2, half, half)

    wgate = jnp.concatenate([block_diag(lru_wa[0]), block_diag(lru_wx[0])], axis=2).astype(BF16)
    zeros4 = jnp.zeros((GDN_HEADS,), F32)
    neg_a = jnp.concatenate([zeros4, jnp.exp(gdn_a_log[0].astype(F32))])
    dtb = jnp.concatenate([zeros4, gdn_dt_bias[0].astype(F32)])
    gate_par = jnp.stack([neg_a, dtb])
    w_router = jnp.zeros((D_MODEL, LANES), F32)
    w_router = w_router.at[:, :N_GROUPS].set(router_group_w[0])
    w_router = w_router.at[:, N_GROUPS:N_GROUPS + N_EXPERTS].set(router_expert_w[0])
    b_router = jnp.zeros((1, LANES), F32)
    b_router = b_router.at[0, :N_GROUPS].set(router_group_b[0])
    b_router = b_router.at[0, N_GROUPS:N_GROUPS + N_EXPERTS].set(router_expert_b[0])
    return {
        "ln1_g": ln1_g[0].reshape(1, D_MODEL), "ln2_g": ln2_g[0].reshape(1, D_MODEL),
        "w_main": w_in0[:, :N_MAIN].astype(BF16),
        "w_gate_col": w_in0[:, N_MAIN:].astype(BF16),
        "w_gate_row": w_in0[:, N_MAIN:].T.astype(BF16),
        "lru_cw": lru_conv_w[0], "lru_cb": lru_conv_b[0].reshape(1, LRU_WIDTH),
        "lru_wgate": wgate,
        "lru_ba": lru_ba[0].reshape(1, LRU_WIDTH), "lru_bx": lru_bx[0].reshape(1, LRU_WIDTH),
        "lru_lam": lru_lambda[0].reshape(1, LRU_WIDTH),
        "gdn_cw": gdn_conv_w[0], "gate_par_col": gate_par, "gate_par_row": gate_par.T,
        "gdn_ng": gdn_norm_g[0].reshape(1, GDN_DV),
        "w_out": w_out[0].astype(BF16),
        "w_router": w_router, "b_router": b_router,
        "moe_w1": moe_w1[0].astype(BF16), "moe_w3": moe_w3[0].astype(BF16),
        "moe_w2": moe_w2[0].astype(BF16),
        "final_g": final_g.reshape(1, D_MODEL),
    }


def _pad_conv_state(s):
    return jnp.pad(s, ((0, 0), (SUBLANES - (CONV_W - 1), 0), (0, 0)))


def _trunk(x, conv_l, h_l, conv_g, s_g, wts, tl_in, bb, tl_tok, rb, td):
    B, L, _ = x.shape
    T = B * L
    c = min(CHUNK, L)
    oa, q, k, v, gz, gcol, grow, nlc, nlh, ngc = _mixer_in(
        x, _pad_conv_state(conv_l), h_l.reshape(B, 1, LRU_WIDTH), _pad_conv_state(conv_g), wts, tl_in)
    og, ns = _gdn(q, k, v, gz, gcol, grow, s_g, wts["gdn_ng"], bb, c)
    x1, h2p, ew, ei, cnt = _mixer_out(oa.reshape(T, LRU_WIDTH), og.reshape(T, GDN_V),
                                      x.reshape(T, D_MODEL), wts, tl_tok)
    pad_end, dest_d, dest_c, block_e, n_used, n_rows = _dispatch_plan(ei, cnt, rb, td, tl_tok)
    xs = _dispatch(pad_end, dest_d, h2p, n_rows, td, rb)
    yb = _moe(block_e, n_used, xs, wts, rb)
    y = _combine(dest_c, x1, ew, wts["final_g"], yb, tl_tok)
    return (y.reshape(B, L, D_MODEL), nlc[None], nlh.reshape(1, B, LRU_WIDTH), ngc[None], ns[None])


def kernel(x_prompt, x_sample, state_lru_conv, state_lru_h, state_gdn_conv, state_gdn_S, ln1_g, ln2_g, w_in, lru_conv_w, lru_conv_b, lru_wa, lru_ba, lru_wx, lru_bx, lru_lambda, gdn_conv_w, gdn_a_log, gdn_dt_bias, gdn_norm_g, w_out, router_group_w, router_group_b, router_expert_w, router_expert_b, moe_w1, moe_w3, moe_w2, final_g):
    wts = _prep_weights(ln1_g, ln2_g, w_in, lru_conv_w, lru_conv_b, lru_wa, lru_ba, lru_wx, lru_bx,
                        lru_lambda, gdn_conv_w, gdn_a_log, gdn_dt_bias, gdn_norm_g, w_out,
                        router_group_w, router_group_b, router_expert_w, router_expert_b,
                        moe_w1, moe_w3, moe_w2, final_g)
    B, L, _ = x_prompt.shape
    dt = x_prompt.dtype
    y_p, p_lc, p_lh, p_gc, p_gs = _trunk(
        x_prompt,
        jnp.zeros((B, CONV_W - 1, LRU_WIDTH), dt), jnp.zeros((B, LRU_WIDTH), dt),
        jnp.zeros((B, CONV_W - 1, GDN_CONV_CH), dt), jnp.zeros((B, GDN_HEADS, GDN_DK, GDN_DV), dt),
        wts, tl_in=256, bb=4, tl_tok=512, rb=256, td=512)
    Bs, Ls, _ = x_sample.shape
    y_s, s_lc, s_lh, s_gc, s_gs = _trunk(
        x_sample, state_lru_conv[0], state_lru_h[0], state_gdn_conv[0], state_gdn_S[0],
        wts, tl_in=Ls, bb=2, tl_tok=Bs * Ls, rb=32, td=Bs * Ls)
    return (y_p, y_s, p_lc, p_lh, p_gc, p_gs, s_lc, s_lh, s_gc, s_gs)
```

```python
import functools

import jax
import jax.numpy as jnp
from jax import lax
from jax.experimental import pallas as pl
from jax.experimental.pallas import tpu as pltpu

F32 = jnp.float32
BF16 = jnp.bfloat16

D_MODEL = 1024
CONV_W = 4
LRU_WIDTH = 512
LRU_BLOCKS = 8
LRU_BLOCK = LRU_WIDTH // LRU_BLOCKS
LRU_C = 8.0
GDN_HEADS = 4
GDN_DK = 128
GDN_DV = 128
GDN_QK = GDN_HEADS * GDN_DK
GDN_V = GDN_HEADS * GDN_DV
GDN_CONV_CH = 2 * GDN_QK + GDN_V
N_GATE = 2 * GDN_HEADS
N_MAIN = 2 * LRU_WIDTH + GDN_CONV_CH + GDN_V
CHUNK = 64
N_GROUPS = 4
EXPERTS_PER_GROUP = 8
N_EXPERTS = N_GROUPS * EXPERTS_PER_GROUP
TOP_K = 2
D_EXPERT = 256
EPS = 1e-6

SUBLANES = 8
LANES = 128
MXU_DIM = 256
VMEM_LIMIT_BYTES = 48 * 1024 * 1024


def _bdot(a, b):
    return jnp.dot(a.astype(BF16), b.astype(BF16), preferred_element_type=F32)


def _bdot_nt(a, b):
    return lax.dot_general(a.astype(BF16), b.astype(BF16), (((1,), (1,)), ((), ())),
                           preferred_element_type=F32)


def _bdot_tn(a, b):
    return lax.dot_general(a.astype(BF16), b.astype(BF16), (((0,), (0,)), ((), ())),
                           preferred_element_type=F32)


def _split2(a):
    hi = a.astype(BF16)
    lo = (a - hi.astype(F32)).astype(BF16)
    return hi, lo


def _dot3(a, b):
    ah, al = _split2(a)
    bh, bl = _split2(b)
    d = functools.partial(jnp.dot, preferred_element_type=F32)
    if 3 * a.shape[1] <= MXU_DIM:
        return d(jnp.concatenate([ah, ah, al], axis=1), jnp.concatenate([bh, bl, bh], axis=0))
    return d(ah, bh) + (d(ah, bl) + d(al, bh))


def _dot_exact_rhs(a, b01):
    b = b01.astype(BF16)
    a1 = a.astype(BF16)
    r1 = a - a1.astype(F32)
    a2 = r1.astype(BF16)
    a3 = (r1 - a2.astype(F32)).astype(BF16)
    d = functools.partial(jnp.dot, preferred_element_type=F32)
    return d(a1, b) + (d(a2, b) + d(a3, b))


def _dot_exact_lhs(a01, b):
    a = a01.astype(BF16)
    b1 = b.astype(BF16)
    r1 = b - b1.astype(F32)
    b2 = r1.astype(BF16)
    b3 = (r1 - b2.astype(F32)).astype(BF16)
    d = functools.partial(jnp.dot, preferred_element_type=F32)
    return d(a, b1) + (d(a, b2) + d(a, b3))


def _sigmoid(x):
    return 1.0 / (1.0 + jnp.exp(-x))


def _softplus(x):
    return jnp.maximum(x, 0.0) + jnp.log(1.0 + jnp.exp(-jnp.abs(x)))


def _silu(x):
    return x * _sigmoid(x)


def _gelu_tanh(x):
    c = 0.7978845608028654
    return 0.5 * x * (1.0 + jnp.tanh(c * (x + 0.044715 * (x * x * x))))


def _rms_scale(x):
    return lax.rsqrt(jnp.mean(x * x, axis=-1, keepdims=True) + EPS)


def _params(sem):
    return pltpu.CompilerParams(dimension_semantics=sem, vmem_limit_bytes=VMEM_LIMIT_BYTES)


def _mixer_in_kernel(x_ref, g1_ref, wmain_ref, wgc_ref, wgr_ref,
                     lcw_ref, lcb_ref, wgate_ref, ba_ref, bx_ref, lam_ref,
                     gcw_ref, gpc_ref, gpr_ref,
                     lconv0_ref, lh0_ref, gconv0_ref,
                     outa_ref, q_ref, k_ref, v_ref, gz_ref, gcol_ref, grow_ref,
                     nlc_ref, nlh_ref, ngc_ref,
                     xl_buf, xg_buf, hcar, acum, bcum, *, tl):
    lt = pl.program_id(1)
    s1, s2, s3 = LRU_WIDTH, 2 * LRU_WIDTH, 2 * LRU_WIDTH + GDN_CONV_CH

    @pl.when(lt == 0)
    def _():
        xl_buf[0:SUBLANES, :] = lconv0_ref[0]
        xg_buf[0:SUBLANES, :] = gconv0_ref[0]
        hcar[...] = lh0_ref[0]

    @pl.when(lt > 0)
    def _():
        xl_buf[0:SUBLANES, :] = xl_buf[tl:tl + SUBLANES, :]
        xg_buf[0:SUBLANES, :] = xg_buf[tl:tl + SUBLANES, :]

    x = x_ref[0]
    hn = (x * _rms_scale(x) * g1_ref[...]).astype(BF16)

    xl_buf[SUBLANES:SUBLANES + tl, :] = jnp.dot(hn, wmain_ref[:, 0:s1], preferred_element_type=F32)
    base = SUBLANES - (CONV_W - 1)

    def causal_conv(buf, w_ref):
        xall = buf[...]
        acc = None
        for j in range(CONV_W):
            back = CONV_W - 1 - j
            shifted = pltpu.roll(xall, back, axis=0) if back else xall
            term = shifted[SUBLANES:, :] * w_ref[j:j + 1, :]
            acc = term if acc is None else acc + term
        return acc

    xc = causal_conv(xl_buf, lcw_ref) + lcb_ref[...]
    nlc_ref[0] = xl_buf[tl + base:tl + SUBLANES, :]

    half = LRU_WIDTH // 2
    pre = [_bdot(xc[:, s * half:(s + 1) * half], wgate_ref[s]) for s in range(2)]
    r_pre = jnp.concatenate([pre[0][:, :half], pre[1][:, :half]], axis=1)
    i_pre = jnp.concatenate([pre[0][:, half:], pre[1][:, half:]], axis=1)
    r = _sigmoid(r_pre + ba_ref[...])
    ig = _sigmoid(i_pre + bx_ref[...])
    log_a = (-LRU_C) * r * _softplus(-lam_ref[...])
    a = jnp.exp(log_a)
    b = jnp.sqrt(1.0 - jnp.exp(2.0 * log_a)) * (ig * xc)

    ng = tl // SUBLANES
    a3 = a.reshape(ng, SUBLANES, LRU_WIDTH)
    b3 = b.reshape(ng, SUBLANES, LRU_WIDTH)
    row = lax.broadcasted_iota(jnp.int32, (ng, SUBLANES, LRU_WIDTH), 1)
    sh = 1
    while sh < SUBLANES:
        a_s = pltpu.roll(a3, sh, axis=1)
        b_s = pltpu.roll(b3, sh, axis=1)
        m = row >= sh
        b3 = jnp.where(m, a3 * b_s + b3, b3)
        a3 = jnp.where(m, a3 * a_s, a3)
        sh *= 2
    h0 = hcar[...]
    if ng % SUBLANES:
        parts = []
        hlast = h0
        for g in range(ng):
            hg = a3[g] * hlast + b3[g]
            parts.append(hg)
            hlast = hg[SUBLANES - 1:SUBLANES, :]
        hs3 = jnp.stack(parts)
    else:
        def group_last(scr, val3):
            val = val3.reshape(tl, LRU_WIDTH)
            nch = LRU_WIDTH // LANES
            for ch in range(nch):
                scr[ch] = val[:, ch * LANES:(ch + 1) * LANES]
            return jnp.concatenate(
                [scr[ch, pl.ds(SUBLANES - 1, ng, stride=SUBLANES), :] for ch in range(nch)], axis=1)

        ag = group_last(acum, a3)
        bg = group_last(bcum, b3)
        grow_i = lax.broadcasted_iota(jnp.int32, (ng, LRU_WIDTH), 0)
        sh = 1
        while sh < ng:
            a_s = pltpu.roll(ag, sh, axis=0)
            b_s = pltpu.roll(bg, sh, axis=0)
            m = grow_i >= sh
            bg = jnp.where(m, ag * b_s + bg, bg)
            ag = jnp.where(m, ag * a_s, ag)
            sh *= 2
        hend = ag * h0 + bg
        hin = jnp.where(grow_i == 0, h0, pltpu.roll(hend, 1, axis=0))
        hs3 = a3 * hin.reshape(ng, 1, LRU_WIDTH) + b3
        hlast = hend[ng - 1:ng, :]
    hcar[...] = hlast
    nlh_ref[0] = hlast
    y = jnp.dot(hn, wmain_ref[:, s1:s2], preferred_element_type=F32)
    outa_ref[0] = (hs3.reshape(tl, LRU_WIDTH) * _gelu_tanh(y)).astype(BF16)

    xg_buf[SUBLANES:SUBLANES + tl, :] = jnp.dot(hn, wmain_ref[:, s2:s3], preferred_element_type=F32)
    ngc_ref[0] = xg_buf[tl + base:tl + SUBLANES, :]
    qkv = _silu(causal_conv(xg_buf, gcw_ref))
    for h in range(GDN_HEADS):
        qh = qkv[:, h * GDN_DK:(h + 1) * GDN_DK]
        kh = qkv[:, GDN_QK + h * GDN_DK:GDN_QK + (h + 1) * GDN_DK]
        qs = lax.rsqrt(jnp.sum(qh * qh, axis=-1, keepdims=True) + EPS) * (GDN_DK ** -0.5)
        ks = lax.rsqrt(jnp.sum(kh * kh, axis=-1, keepdims=True) + EPS)
        q_ref[0, :, h * GDN_DK:(h + 1) * GDN_DK] = qh * qs
        k_ref[0, :, h * GDN_DK:(h + 1) * GDN_DK] = kh * ks
    v_ref[0] = qkv[:, 2 * GDN_QK:]
    gz_ref[0] = _silu(jnp.dot(hn, wmain_ref[:, s3:], preferred_element_type=F32))

    gc = jnp.dot(hn, wgc_ref[...], preferred_element_type=F32)
    lane = lax.broadcasted_iota(jnp.int32, gc.shape, 1)
    gcol_ref[0] = jnp.where(lane < GDN_HEADS, _sigmoid(gc),
                            -gpc_ref[0:1, :] * _softplus(gc + gpc_ref[1:2, :]))
    gr = lax.dot_general(wgr_ref[...], hn, (((1,), (1,)), ((), ())), preferred_element_type=F32)
    sub = lax.broadcasted_iota(jnp.int32, gr.shape, 0)
    grow_ref[0] = jnp.where(sub < GDN_HEADS, _sigmoid(gr),
                            -gpr_ref[:, 0:1] * _softplus(gr + gpr_ref[:, 1:2]))


def _mixer_in(x, conv_l8, h_l, conv_g8, wts, tl):
    B, L, _ = x.shape
    nl = L // tl
    const2 = lambda b, l: (0, 0)
    const3 = lambda b, l: (0, 0, 0)
    per_b3 = lambda b, l: (b, 0, 0)
    tile3 = lambda b, l: (b, l, 0)
    in_specs = [
        pl.BlockSpec((1, tl, D_MODEL), tile3),
        pl.BlockSpec((1, D_MODEL), const2),
        pl.BlockSpec((D_MODEL, N_MAIN), const2),
        pl.BlockSpec((D_MODEL, N_GATE), const2),
        pl.BlockSpec((N_GATE, D_MODEL), const2),
        pl.BlockSpec((CONV_W, LRU_WIDTH), const2),
        pl.BlockSpec((1, LRU_WIDTH), const2),
        pl.BlockSpec((2, LRU_WIDTH // 2, LRU_WIDTH), const3),
        pl.BlockSpec((1, LRU_WIDTH), const2),
        pl.BlockSpec((1, LRU_WIDTH), const2),
        pl.BlockSpec((1, LRU_WIDTH), const2),
        pl.BlockSpec((CONV_W, GDN_CONV_CH), const2),
        pl.BlockSpec((2, N_GATE), const2),
        pl.BlockSpec((N_GATE, 2), const2),
        pl.BlockSpec((1, SUBLANES, LRU_WIDTH), per_b3),
        pl.BlockSpec((1, 1, LRU_WIDTH), per_b3),
        pl.BlockSpec((1, SUBLANES, GDN_CONV_CH), per_b3),
    ]
    out_shape = (
        jax.ShapeDtypeStruct((B, L, LRU_WIDTH), BF16),
        jax.ShapeDtypeStruct((B, L, GDN_QK), F32),
        jax.ShapeDtypeStruct((B, L, GDN_QK), F32),
        jax.ShapeDtypeStruct((B, L, GDN_V), F32),
        jax.ShapeDtypeStruct((B, L, GDN_V), F32),
        jax.ShapeDtypeStruct((B, L, N_GATE), F32),
        jax.ShapeDtypeStruct((B, N_GATE, L), F32),
        jax.ShapeDtypeStruct((B, CONV_W - 1, LRU_WIDTH), F32),
        jax.ShapeDtypeStruct((B, 1, LRU_WIDTH), F32),
        jax.ShapeDtypeStruct((B, CONV_W - 1, GDN_CONV_CH), F32),
    )
    out_specs = (
        pl.BlockSpec((1, tl, LRU_WIDTH), tile3),
        pl.BlockSpec((1, tl, GDN_QK), tile3),
        pl.BlockSpec((1, tl, GDN_QK), tile3),
        pl.BlockSpec((1, tl, GDN_V), tile3),
        pl.BlockSpec((1, tl, GDN_V), tile3),
        pl.BlockSpec((1, tl, N_GATE), tile3),
        pl.BlockSpec((1, N_GATE, tl), lambda b, l: (b, 0, l)),
        pl.BlockSpec((1, CONV_W - 1, LRU_WIDTH), per_b3),
        pl.BlockSpec((1, 1, LRU_WIDTH), per_b3),
        pl.BlockSpec((1, CONV_W - 1, GDN_CONV_CH), per_b3),
    )
    scratch = [
        pltpu.VMEM((tl + SUBLANES, LRU_WIDTH), F32),
        pltpu.VMEM((tl + SUBLANES, GDN_CONV_CH), F32),
        pltpu.VMEM((1, LRU_WIDTH), F32),
        pltpu.VMEM((LRU_WIDTH // LANES, tl, LANES), F32),
        pltpu.VMEM((LRU_WIDTH // LANES, tl, LANES), F32),
    ]
    return pl.pallas_call(
        functools.partial(_mixer_in_kernel, tl=tl),
        grid=(B, nl), in_specs=in_specs, out_specs=out_specs, out_shape=out_shape,
        scratch_shapes=scratch, compiler_params=_params(("arbitrary", "arbitrary")),
        name="mixer_in",
    )(x, wts["ln1_g"], wts["w_main"], wts["w_gate_col"], wts["w_gate_row"],
      wts["lru_cw"], wts["lru_cb"], wts["lru_wgate"], wts["lru_ba"], wts["lru_bx"], wts["lru_lam"],
      wts["gdn_cw"], wts["gate_par_col"], wts["gate_par_row"],
      conv_l8, h_l, conv_g8)


def _tri_inverse_all(lms, c):
    r = lax.broadcasted_iota(jnp.int32, (c, LANES), 0)
    lane = lax.broadcasted_iota(jnp.int32, (c, LANES), 1)
    col = jnp.bitwise_and(lane, c - 1)
    eye = (r == col).astype(F32)
    odd_block = jnp.bitwise_and(jnp.right_shift(lane, c.bit_length() - 1), 1) == 1

    def same_block(n):
        sh = n.bit_length() - 1
        return jnp.right_shift(r, sh) == jnp.right_shift(col, sh)

    def pieces(a, b):
        ah = a.astype(BF16)
        ah32 = ah.astype(F32)
        mixed = jnp.where(odd_block, a - ah32, ah32).astype(BF16)
        lhs = (mixed if 3 * c <= LANES else jnp.concatenate([mixed, ah], axis=1))[:, :3 * c]
        bh = b.astype(BF16)
        bl = (b - bh.astype(F32)).astype(BF16)
        return lhs, jnp.concatenate([bh, bh, bl], axis=0)

    def dot3_all(xs, ys):
        out = []
        for i in range(0, len(xs), 2):
            (l0, r0), (l1, r1) = pieces(xs[i], ys[i]), pieces(xs[i + 1], ys[i + 1])
            both = jnp.dot(jnp.concatenate([l0, l1], axis=0), jnp.concatenate([r0, r1], axis=1),
                           preferred_element_type=F32)
            out += [both[:c, :LANES], both[c:, LANES:]]
        return out

    blk8 = same_block(SUBLANES)
    l8 = [jnp.where(blk8, lm, 0.0) for lm in lms]
    l2 = dot3_all(l8, l8)
    n1 = [eye - a for a in l8]
    p = [a + b for a, b in zip(n1, dot3_all(n1, l2))]
    l4 = dot3_all(l2, l2)
    p = [a + b for a, b in zip(p, dot3_all(p, l4))]
    n = SUBLANES
    while n < c:
        off = same_block(2 * n) & jnp.logical_not(same_block(n))
        lo = [jnp.where(off, lm, 0.0) for lm in lms]
        t = dot3_all(p, lo)
        p = [a - b for a, b in zip(p, dot3_all(t, p))]
        n *= 2
    return p


def _gdn_kernel(q_ref, k_ref, v_ref, gz_ref, gcol_ref, grow_ref, s0_ref, ng_ref,
                o_ref, sout_ref, s_scr, *, bb, c, cps):
    ct = pl.program_id(1)

    @pl.when(ct == 0)
    def _():
        s_scr[...] = s0_ref[...]

    reps = LANES // c
    r = lax.broadcasted_iota(jnp.int32, (c, LANES), 0)
    col = jnp.bitwise_and(lax.broadcasted_iota(jnp.int32, (c, LANES), 1), c - 1)
    lower = r >= col
    strict = r > col
    rs = lax.broadcasted_iota(jnp.int32, (c, c), 0)
    cs = lax.broadcasted_iota(jnp.int32, (c, c), 1)
    tri_l = (rs >= cs).astype(F32)
    tri_u = jnp.where(r <= col, 1.0, 0.0)

    def hs(h):
        return slice(h * GDN_DK, (h + 1) * GDN_DK)

    def rows(j):
        return slice(j * c, (j + 1) * c)

    chains = [(j, b, h) for j in range(cps) for b in range(bb) for h in range(GDN_HEADS)]
    pairs = [(j, b) for j in range(cps) for b in range(bb)]
    gates_c = {jb: gcol_ref[jb[1], rows(jb[0]), :] for jb in pairs}
    gcum_c = {jb: _dot_exact_lhs(tri_l, gates_c[jb]) for jb in pairs}
    gcum_r = {jb: _dot_exact_rhs(grow_ref[jb[1], jb[0]], tri_u) for jb in pairs}
    beta = [gates_c[(j, b)][:, h:h + 1] for j, b, h in chains]
    gc_c = [gcum_c[(j, b)][:, GDN_HEADS + h:GDN_HEADS + h + 1] for j, b, h in chains]
    gc_r = [gcum_r[(j, b)][GDN_HEADS + h:GDN_HEADS + h + 1, :] for j, b, h in chains]
    decay = [jnp.where(lower, jnp.exp(jnp.where(lower, a - b, 0.0)), 0.0) for a, b in zip(gc_c, gc_r)]
    e_c = [jnp.exp(a) for a in gc_c]
    kb = [k_ref[b, rows(j), hs(h)] * bt for (j, b, h), bt in zip(chains, beta)]
    kk = [_bdot_nt(jnp.concatenate([a, q_ref[b, rows(j), hs(h)]], axis=0),
                   jnp.concatenate([k_ref[b, rows(j), hs(h)]] * reps, axis=0))
          for (j, b, h), a in zip(chains, kb)]
    lmat = [jnp.where(strict, x[:c] * d, 0.0) for x, d in zip(kk, decay)]
    attn = [x[c:, :c] * d[:, :c] for x, d in zip(kk, decay)]
    t_inv = _tri_inverse_all(lmat, c)
    uw = [_bdot(t[:, :c], jnp.concatenate([v_ref[b, rows(j), hs(h)] * bt, a * e], axis=1))
          for (j, b, h), t, bt, a, e in zip(chains, t_inv, beta, kb, e_c)]
    per = bb * GDN_HEADS
    for j in range(cps):
        sel = range(j * per, (j + 1) * per)
        ws = [_bdot(jnp.concatenate([uw[i][:, GDN_DV:],
                                     q_ref[chains[i][1], rows(j), hs(chains[i][2])] * e_c[i]], axis=0),
                    s_scr[chains[i][1], chains[i][2]]) for i in sel]
        v_new = [uw[i][:, :GDN_DV] - x[:c] for i, x in zip(sel, ws)]
        o = [x[c:] + _bdot(attn[i], vn) for i, x, vn in zip(sel, ws, v_new)]
        for i, vn in zip(sel, v_new):
            _, b, h = chains[i]
            g_last = gc_c[i][c - 1:c, :]
            k_dec = k_ref[b, rows(j), hs(h)] * jnp.exp(g_last - gc_c[i])
            s_scr[b, h] = s_scr[b, h] * jnp.exp(g_last) + _bdot_tn(k_dec, vn)
        for i, x in zip(sel, o):
            _, b, h = chains[i]
            on = x * _rms_scale(x) * ng_ref[...]
            o_ref[b, rows(j), hs(h)] = (on * gz_ref[b, rows(j), hs(h)]).astype(BF16)

    sout_ref[...] = s_scr[...]


def _gdn(q, k, v, gz, gcol, grow, s0, ng, bb, c, cps):
    B, L, _ = q.shape
    nc = L // c
    tl = cps * c
    tile = lambda i, t: (i, t, 0)
    per_b4 = lambda i, t: (i, 0, 0, 0)
    in_specs = [
        pl.BlockSpec((bb, tl, GDN_QK), tile),
        pl.BlockSpec((bb, tl, GDN_QK), tile),
        pl.BlockSpec((bb, tl, GDN_V), tile),
        pl.BlockSpec((bb, tl, GDN_V), tile),
        pl.BlockSpec((bb, tl, N_GATE), tile),
        pl.BlockSpec((bb, cps, N_GATE, c), lambda i, t: (i, t, 0, 0)),
        pl.BlockSpec((bb, GDN_HEADS, GDN_DK, GDN_DV), per_b4),
        pl.BlockSpec((1, GDN_DV), lambda i, t: (0, 0)),
    ]
    out_shape = (jax.ShapeDtypeStruct((B, L, GDN_V), BF16),
                 jax.ShapeDtypeStruct((B, GDN_HEADS, GDN_DK, GDN_DV), F32))
    out_specs = (pl.BlockSpec((bb, tl, GDN_V), tile),
                 pl.BlockSpec((bb, GDN_HEADS, GDN_DK, GDN_DV), per_b4))
    return pl.pallas_call(
        functools.partial(_gdn_kernel, bb=bb, c=c, cps=cps),
        grid=(B // bb, nc // cps), in_specs=in_specs, out_specs=out_specs, out_shape=out_shape,
        scratch_shapes=[pltpu.VMEM((bb, GDN_HEADS, GDN_DK, GDN_DV), F32)],
        compiler_params=_params(("arbitrary", "arbitrary")),
        name="gdn",
    )(q, k, v, gz, gcol, grow.reshape(B, N_GATE, nc, c).transpose(0, 2, 1, 3), s0, ng)


def _mixer_out_kernel(oa_ref, og_ref, x_ref, wout_ref, g2_ref, wr_ref, br_ref,
                      x1_ref, h2p_ref, ew_ref, ei_ref, cnt_ref, carry, *, tl):
    @pl.when(pl.program_id(0) == 0)
    def _():
        carry[...] = jnp.zeros_like(carry)

    mix = (jnp.dot(oa_ref[...], wout_ref[0:LRU_WIDTH, :], preferred_element_type=F32)
           + jnp.dot(og_ref[...], wout_ref[LRU_WIDTH:, :], preferred_element_type=F32))
    x1 = x_ref[...] + mix
    x1_ref[...] = x1
    h2 = x1 * _rms_scale(x1) * g2_ref[...]
    bits = pltpu.bitcast(h2.astype(BF16).astype(F32), jnp.uint32)
    h2p_ref[...] = (bits[:, :D_MODEL // 2] & jnp.uint32(0xFFFF0000)) | (bits[:, D_MODEL // 2:] >> 16)
    logits = _bdot(h2, wr_ref[...]) + br_ref[...]
    lane = lax.broadcasted_iota(jnp.int32, logits.shape, 1)
    lane_f = lane.astype(F32)
    neg = jnp.float32(-jnp.inf)
    big = jnp.float32(1e9)
    gmask = lane < N_GROUPS
    gl = jnp.where(gmask, logits, neg)
    gmax = jnp.max(gl, axis=1, keepdims=True)
    grp = jnp.min(jnp.where(gl == gmax, lane_f, big), axis=1, keepdims=True)
    p_grp = 1.0 / jnp.sum(jnp.where(gmask, jnp.exp(gl - gmax), 0.0), axis=1, keepdims=True)
    egrp = jnp.right_shift(lane - N_GROUPS, EXPERTS_PER_GROUP.bit_length() - 1).astype(F32)
    sel = (lane >= N_GROUPS) & (lane < N_GROUPS + N_EXPERTS) & (egrp == grp)
    el = jnp.where(sel, logits, neg)
    v1 = jnp.max(el, axis=1, keepdims=True)
    i1 = jnp.min(jnp.where(el == v1, lane_f, big), axis=1, keepdims=True)
    el2 = jnp.where(lane_f == i1, neg, el)
    v2 = jnp.max(el2, axis=1, keepdims=True)
    i2 = jnp.min(jnp.where(el2 == v2, lane_f, big), axis=1, keepdims=True)
    e2 = jnp.exp(v2 - v1)
    den = 1.0 / (1.0 + e2)
    w1 = den * p_grp
    w2 = (e2 * den) * p_grp
    ew_ref[...] = jnp.where(lane == 0, w1, jnp.where(lane == 1, w2, 0.0))

    ex1 = i1 - N_GROUPS
    ex2 = i2 - N_GROUPS
    hit1 = lane_f == ex1
    hit2 = lane_f == ex2
    oh = jnp.concatenate([jnp.where(hit1, 1.0, 0.0), jnp.where(hit2, 1.0, 0.0)], axis=1)
    rr = lax.broadcasted_iota(jnp.int32, (tl, tl), 0)
    cc = lax.broadcasted_iota(jnp.int32, (tl, tl), 1)
    before = jnp.dot(jnp.where(rr > cc, 1.0, 0.0).astype(BF16), oh.astype(BF16),
                     preferred_element_type=F32)
    tot = jnp.sum(oh, axis=0, keepdims=True)
    tot1, tot2 = tot[:, :LANES], tot[:, LANES:]
    c0 = carry[...]
    rank1 = jnp.sum(jnp.where(hit1, c0 + before[:, :LANES], 0.0), axis=1, keepdims=True)
    rank2 = jnp.sum(jnp.where(hit2, (c0 + tot1) + before[:, LANES:], 0.0), axis=1, keepdims=True)
    c1 = c0 + (tot1 + tot2)
    carry[...] = c1
    cnt_ref[...] = c1
    ei_ref[...] = jnp.where(lane == 0, ex1, jnp.where(lane == 1, ex2, jnp.where(
        lane == 2, rank1, jnp.where(lane == 3, rank2, 0.0)))).astype(jnp.int32)


def _mixer_out(oa, og, x, wts, tl):
    T = x.shape[0]
    tile = lambda i: (i, 0)
    const = lambda i: (0, 0)
    in_specs = [
        pl.BlockSpec((tl, LRU_WIDTH), tile),
        pl.BlockSpec((tl, GDN_V), tile),
        pl.BlockSpec((tl, D_MODEL), tile),
        pl.BlockSpec((D_MODEL, D_MODEL), const),
        pl.BlockSpec((1, D_MODEL), const),
        pl.BlockSpec((D_MODEL, LANES), const),
        pl.BlockSpec((1, LANES), const),
    ]
    out_shape = (jax.ShapeDtypeStruct((T, D_MODEL), F32),
                 jax.ShapeDtypeStruct((T, D_MODEL // 2), jnp.uint32),
                 jax.ShapeDtypeStruct((T, LANES), F32),
                 jax.ShapeDtypeStruct((T, LANES), jnp.int32),
                 jax.ShapeDtypeStruct((1, LANES), F32))
    out_specs = (pl.BlockSpec((tl, D_MODEL), tile), pl.BlockSpec((tl, D_MODEL // 2), tile),
                 pl.BlockSpec((tl, LANES), tile), pl.BlockSpec((tl, LANES), tile),
                 pl.BlockSpec((1, LANES), const))
    return pl.pallas_call(
        functools.partial(_mixer_out_kernel, tl=tl), grid=(T // tl,), in_specs=in_specs,
        out_specs=out_specs, out_shape=out_shape, scratch_shapes=[pltpu.VMEM((1, LANES), F32)],
        compiler_params=_params(("arbitrary",)), name="mixer_out",
    )(oa, og, x, wts["w_out"], wts["ln2_g"], wts["w_router"], wts["b_router"])


def _row_copy(src, src_row, dst, dst_row, sem):
    return pltpu.make_async_copy(src.at[pl.ds(src_row, 1), :], dst.at[pl.ds(dst_row, 1), :], sem)


def _dispatch_kernel(pe_ref, dest_ref, h2p_ref, xs_hbm, zbuf, sem_z, sem_r, *, td, rb, n_blocks):
    i = pl.program_id(0)

    def tail_copy(e):
        end = pe_ref[e]
        start = pl.multiple_of(end - rb, rb)
        return pltpu.make_async_copy(zbuf, xs_hbm.at[pl.ds(start, rb), :], sem_z)

    def nonempty(e):
        return pe_ref[e] > (pe_ref[e - 1] if e > 0 else 0)

    def unused_copy(j):
        return pltpu.make_async_copy(zbuf, xs_hbm.at[pl.ds(pl.multiple_of(j * rb, rb), rb), :], sem_z)

    @pl.when(i == 0)
    def _():
        zbuf[...] = jnp.zeros_like(zbuf)
        for e in range(N_EXPERTS):
            pl.when(nonempty(e))(lambda e=e: tail_copy(e).start())
        for e in range(N_EXPERTS):
            pl.when(nonempty(e))(lambda e=e: tail_copy(e).wait())
        first_unused = lax.shift_right_logical(pe_ref[N_EXPERTS - 1], rb.bit_length() - 1)
        lax.fori_loop(first_unused, n_blocks, lambda j, c: (unused_copy(j).start(), c)[1], 0)
        lax.fori_loop(first_unused, n_blocks, lambda j, c: (unused_copy(j).wait(), c)[1], 0)

    def copies(r):
        return [_row_copy(h2p_ref, r, xs_hbm, dest_ref[0, 0, k * td + r], sem_r)
                for k in range(TOP_K)]

    for r in range(td):
        for k, cp in enumerate(copies(r)):
            cp.start(priority=k)
    for r in range(td):
        for cp in copies(r):
            cp.wait()


def _dispatch(pad_end, dest_tiles, h2p, n_rows, td, rb):
    T = h2p.shape[0]
    grid_spec = pltpu.PrefetchScalarGridSpec(
        num_scalar_prefetch=1, grid=(T // td,),
        in_specs=[pl.BlockSpec((1, 1, TOP_K * td), lambda i, pe: (i, 0, 0), memory_space=pltpu.SMEM),
                  pl.BlockSpec((td, D_MODEL // 2), lambda i, pe: (i, 0))],
        out_specs=pl.BlockSpec(memory_space=pl.ANY),
        scratch_shapes=[pltpu.VMEM((rb, D_MODEL // 2), jnp.uint32),
                        pltpu.SemaphoreType.DMA(()), pltpu.SemaphoreType.DMA(())],
    )
    return pl.pallas_call(
        functools.partial(_dispatch_kernel, td=td, rb=rb, n_blocks=n_rows // rb), grid_spec=grid_spec,
        out_shape=jax.ShapeDtypeStruct((n_rows, D_MODEL // 2), jnp.uint32),
        compiler_params=_params(("arbitrary",)), name="dispatch",
    )(pad_end, dest_tiles, h2p)


def _moe_kernel(be_ref, nu_ref, xs_ref, w1_ref, w3_ref, w2_ref, yb_ref):
    del be_ref
    half = D_MODEL // 2

    @pl.when(pl.program_id(0) >= nu_ref[0])
    def _():
        yb_ref[...] = jnp.zeros_like(yb_ref)

    @pl.when(pl.program_id(0) < nu_ref[0])
    def _():
        u = xs_ref[...]
        xa = pltpu.bitcast(u & jnp.uint32(0xFFFF0000), F32).astype(BF16)
        xb = pltpu.bitcast(u << 16, F32).astype(BF16)

        def proj(w_ref):
            return (jnp.dot(xa, w_ref[0, 0:half, :], preferred_element_type=F32)
                    + jnp.dot(xb, w_ref[0, half:, :], preferred_element_type=F32))

        hmid = _silu(proj(w1_ref)) * proj(w3_ref)
        yb_ref[...] = jnp.dot(hmid.astype(BF16), w2_ref[0], preferred_element_type=F32)


def _moe(block_e, n_used, xs, wts, rb):
    n_blocks = block_e.shape[0]

    grid_spec = pltpu.PrefetchScalarGridSpec(
        num_scalar_prefetch=2, grid=(n_blocks,),
        in_specs=[
            pl.BlockSpec((rb, D_MODEL // 2), lambda i, be, nu: (i, 0)),
            pl.BlockSpec((1, D_MODEL, D_EXPERT), lambda i, be, nu: (be[i], 0, 0)),
            pl.BlockSpec((1, D_MODEL, D_EXPERT), lambda i, be, nu: (be[i], 0, 0)),
            pl.BlockSpec((1, D_EXPERT, D_MODEL), lambda i, be, nu: (be[i], 0, 0)),
        ],
        out_specs=pl.BlockSpec((rb, D_MODEL), lambda i, be, nu: (i, 0)),
    )
    return pl.pallas_call(
        _moe_kernel, grid_spec=grid_spec,
        out_shape=jax.ShapeDtypeStruct((n_blocks * rb, D_MODEL), F32),
        compiler_params=_params(("arbitrary",)), name="moe",
    )(block_e, n_used, xs, wts["moe_w1"], wts["moe_w3"], wts["moe_w2"])


def _dispatch_plan(ei, cnt, rb, td, tc):
    T = ei.shape[0]
    n_blocks = -(-(T * TOP_K + N_EXPERTS * (rb - 1)) // rb)
    counts = cnt[0, :N_EXPERTS].astype(jnp.int32)
    padded = (counts + rb - 1) // rb * rb
    pad_end = jnp.cumsum(padded).astype(jnp.int32)
    pad_start = pad_end - padded
    e = ei[:, 0:TOP_K]
    rank = ei[:, TOP_K:2 * TOP_K]
    onehot = e[:, :, None] == jnp.arange(N_EXPERTS, dtype=jnp.int32)
    dest = rank + jnp.sum(jnp.where(onehot, pad_start, 0), axis=-1)

    def tiles(t):
        return dest.T.reshape(TOP_K, T // t, t).transpose(1, 0, 2).reshape(T // t, 1, TOP_K * t)

    blk_start = jnp.arange(n_blocks, dtype=jnp.int32) * rb
    block_e = jnp.minimum(jnp.sum(pad_end[None, :] <= blk_start[:, None], axis=1),
                          N_EXPERTS - 1).astype(jnp.int32)
    n_used = (pad_end[N_EXPERTS - 1:] // rb).astype(jnp.int32)
    return pad_end, tiles(td), tiles(tc), block_e, n_used, n_blocks * rb


def _combine_kernel(dest_ref, x1_ref, ew_ref, gf_ref, yb_hbm, y_ref, ybuf, sem, *, tc):
    def copies(r):
        return [pltpu.make_async_copy(yb_hbm.at[pl.ds(dest_ref[0, 0, k * tc + r], 1), :],
                                      ybuf.at[k, pl.ds(r, 1), :], sem) for k in range(TOP_K)]

    for r in range(tc):
        for k, cp in enumerate(copies(r)):
            cp.start(priority=k)
    for r in range(tc):
        for cp in copies(r):
            cp.wait()
    w = ew_ref[...]
    moe = w[:, 0:1] * ybuf[0] + w[:, 1:2] * ybuf[1]
    x2 = x1_ref[...] + moe
    y_ref[...] = x2 * _rms_scale(x2) * gf_ref[...]


def _combine(dest_tiles, x1, ew, final_g, yb, tc):
    T = x1.shape[0]
    tile = lambda i: (i, 0)
    return pl.pallas_call(
        functools.partial(_combine_kernel, tc=tc), grid=(T // tc,),
        in_specs=[pl.BlockSpec((1, 1, TOP_K * tc), lambda i: (i, 0, 0), memory_space=pltpu.SMEM),
                  pl.BlockSpec((tc, D_MODEL), tile), pl.BlockSpec((tc, LANES), tile),
                  pl.BlockSpec((1, D_MODEL), lambda i: (0, 0)),
                  pl.BlockSpec(memory_space=pl.ANY)],
        out_specs=pl.BlockSpec((tc, D_MODEL), tile),
        out_shape=jax.ShapeDtypeStruct((T, D_MODEL), F32),
        scratch_shapes=[pltpu.VMEM((TOP_K, tc, D_MODEL), F32), pltpu.SemaphoreType.DMA(())],
        compiler_params=_params(("arbitrary",)), name="combine",
    )(dest_tiles, x1, ew, final_g, yb)


def _prep_weights(ln1_g, ln2_g, w_in, lru_conv_w, lru_conv_b, lru_wa, lru_ba, lru_wx, lru_bx,
                  lru_lambda, gdn_conv_w, gdn_a_log, gdn_dt_bias, gdn_norm_g, w_out,
                  router_group_w, router_group_b, router_expert_w, router_expert_b,
                  moe_w1, moe_w3, moe_w2, final_g):
    w_in0 = w_in[0]
    half = LRU_WIDTH // 2
    per_half = half // LRU_BLOCK

    def block_diag(w):
        w4 = w.reshape(2, per_half, LRU_BLOCK, LRU_BLOCK)
        eye = jnp.eye(per_half, dtype=w.dtype)
        return jnp.einsum("snij,nm->snimj", w4, eye).reshape(2, half, half)

    wgate = jnp.concatenate([block_diag(lru_wa[0]), block_diag(lru_wx[0])], axis=2).astype(BF16)
    zeros4 = jnp.zeros((GDN_HEADS,), F32)
    neg_a = jnp.concatenate([zeros4, jnp.exp(gdn_a_log[0].astype(F32))])
    dtb = jnp.concatenate([zeros4, gdn_dt_bias[0].astype(F32)])
    gate_par = jnp.stack([neg_a, dtb])
    w_router = jnp.zeros((D_MODEL, LANES), F32)
    w_router = w_router.at[:, :N_GROUPS].set(router_group_w[0])
    w_router = w_router.at[:, N_GROUPS:N_GROUPS + N_EXPERTS].set(router_expert_w[0])
    b_router = jnp.zeros((1, LANES), F32)
    b_router = b_router.at[0, :N_GROUPS].set(router_group_b[0])
    b_router = b_router.at[0, N_GROUPS:N_GROUPS + N_EXPERTS].set(router_expert_b[0])
    return {
        "ln1_g": ln1_g[0].reshape(1, D_MODEL), "ln2_g": ln2_g[0].reshape(1, D_MODEL),
        "w_main": w_in0[:, :N_MAIN].astype(BF16),
        "w_gate_col": w_in0[:, N_MAIN:].astype(BF16),
        "w_gate_row": w_in0[:, N_MAIN:].T.astype(BF16),
        "lru_cw": lru_conv_w[0], "lru_cb": lru_conv_b[0].reshape(1, LRU_WIDTH),
        "lru_wgate": wgate,
        "lru_ba": lru_ba[0].reshape(1, LRU_WIDTH), "lru_bx": lru_bx[0].reshape(1, LRU_WIDTH),
        "lru_lam": lru_lambda[0].reshape(1, LRU_WIDTH),
        "gdn_cw": gdn_conv_w[0], "gate_par_col": gate_par, "gate_par_row": gate_par.T,
        "gdn_ng": gdn_norm_g[0].reshape(1, GDN_DV),
        "w_out": w_out[0].astype(BF16),
        "w_router": w_router, "b_router": b_router,
        "moe_w1": moe_w1[0].astype(BF16), "moe_w3": moe_w3[0].astype(BF16),
        "moe_w2": moe_w2[0].astype(BF16),
        "final_g": final_g.reshape(1, D_MODEL),
    }


def _pad_conv_state(s):
    return jnp.pad(s, ((0, 0), (SUBLANES - (CONV_W - 1), 0), (0, 0)))


def _trunk(x, conv_l, h_l, conv_g, s_g, wts, tl_in, bb, tl_tok, rb, td):
    B, L, _ = x.shape
    T = B * L
    c = min(CHUNK, L)
    oa, q, k, v, gz, gcol, grow, nlc, nlh, ngc = _mixer_in(
        x, _pad_conv_state(conv_l), h_l.reshape(B, 1, LRU_WIDTH), _pad_conv_state(conv_g), wts, tl_in)
    og, ns = _gdn(q, k, v, gz, gcol, grow, s_g, wts["gdn_ng"], bb, c, min(2, L // c))
    x1, h2p, ew, ei, cnt = _mixer_out(oa.reshape(T, LRU_WIDTH), og.reshape(T, GDN_V),
                                      x.reshape(T, D_MODEL), wts, tl_tok)
    pad_end, dest_d, dest_c, block_e, n_used, n_rows = _dispatch_plan(ei, cnt, rb, td, tl_tok)
    xs = _dispatch(pad_end, dest_d, h2p, n_rows, td, rb)
    yb = _moe(block_e, n_used, xs, wts, rb)
    y = _combine(dest_c, x1, ew, wts["final_g"], yb, tl_tok)
    return (y.reshape(B, L, D_MODEL), nlc[None], nlh.reshape(1, B, LRU_WIDTH), ngc[None], ns[None])


def kernel(x_prompt, x_sample, state_lru_conv, state_lru_h, state_gdn_conv, state_gdn_S, ln1_g, ln2_g, w_in, lru_conv_w, lru_conv_b, lru_wa, lru_ba, lru_wx, lru_bx, lru_lambda, gdn_conv_w, gdn_a_log, gdn_dt_bias, gdn_norm_g, w_out, router_group_w, router_group_b, router_expert_w, router_expert_b, moe_w1, moe_w3, moe_w2, final_g):
    wts = _prep_weights(ln1_g, ln2_g, w_in, lru_conv_w, lru_conv_b, lru_wa, lru_ba, lru_wx, lru_bx,
                        lru_lambda, gdn_conv_w, gdn_a_log, gdn_dt_bias, gdn_norm_g, w_out,
                        router_group_w, router_group_b, router_expert_w, router_expert_b,
                        moe_w1, moe_w3, moe_w2, final_g)
    B, L, _ = x_prompt.shape
    dt = x_prompt.dtype
    y_p, p_lc, p_lh, p_gc, p_gs = _trunk(
        x_prompt,
        jnp.zeros((B, CONV_W - 1, LRU_WIDTH), dt), jnp.zeros((B, LRU_WIDTH), dt),
        jnp.zeros((B, CONV_W - 1, GDN_CONV_CH), dt), jnp.zeros((B, GDN_HEADS, GDN_DK, GDN_DV), dt),
        wts, tl_in=256, bb=4, tl_tok=512, rb=512, td=512)
    Bs, Ls, _ = x_sample.shape
    y_s, s_lc, s_lh, s_gc, s_gs = _trunk(
        x_sample, state_lru_conv[0], state_lru_h[0], state_gdn_conv[0], state_gdn_S[0],
        wts, tl_in=Ls, bb=2, tl_tok=Bs * Ls, rb=32, td=Bs * Ls)
    return (y_p, y_s, p_lc, p_lh, p_gc, p_gs, s_lc, s_lh, s_gc, s_gs)
```

```python
import functools

import jax
import jax.numpy as jnp
from jax import lax
from jax.experimental import pallas as pl
from jax.experimental.pallas import tpu as pltpu

F32 = jnp.float32
BF16 = jnp.bfloat16

D_MODEL = 1024
CONV_W = 4
LRU_WIDTH = 512
LRU_BLOCKS = 8
LRU_BLOCK = LRU_WIDTH // LRU_BLOCKS
LRU_C = 8.0
GDN_HEADS = 4
GDN_DK = 128
GDN_DV = 128
GDN_QK = GDN_HEADS * GDN_DK
GDN_V = GDN_HEADS * GDN_DV
GDN_CONV_CH = 2 * GDN_QK + GDN_V
N_GATE = 2 * GDN_HEADS
N_MAIN = 2 * LRU_WIDTH + GDN_CONV_CH + GDN_V
CHUNK = 64
N_GROUPS = 4
EXPERTS_PER_GROUP = 8
N_EXPERTS = N_GROUPS * EXPERTS_PER_GROUP
TOP_K = 2
D_EXPERT = 256
EPS = 1e-6

SUBLANES = 8
LANES = 128
MXU_DIM = 256
VMEM_LIMIT_BYTES = 48 * 1024 * 1024


def _bdot(a, b):
    return jnp.dot(a.astype(BF16), b.astype(BF16), preferred_element_type=F32)


def _bdot_nt(a, b):
    return lax.dot_general(a.astype(BF16), b.astype(BF16), (((1,), (1,)), ((), ())),
                           preferred_element_type=F32)


def _bdot_tn(a, b):
    return lax.dot_general(a.astype(BF16), b.astype(BF16), (((0,), (0,)), ((), ())),
                           preferred_element_type=F32)


def _split2(a):
    hi = a.astype(BF16)
    lo = (a - hi.astype(F32)).astype(BF16)
    return hi, lo


def _dot3(a, b):
    ah, al = _split2(a)
    bh, bl = _split2(b)
    d = functools.partial(jnp.dot, preferred_element_type=F32)
    if 3 * a.shape[1] <= MXU_DIM:
        return d(jnp.concatenate([ah, ah, al], axis=1), jnp.concatenate([bh, bl, bh], axis=0))
    return d(ah, bh) + (d(ah, bl) + d(al, bh))


def _dot_exact_rhs(a, b01):
    b = b01.astype(BF16)
    a1 = a.astype(BF16)
    r1 = a - a1.astype(F32)
    a2 = r1.astype(BF16)
    a3 = (r1 - a2.astype(F32)).astype(BF16)
    d = functools.partial(jnp.dot, preferred_element_type=F32)
    return d(a1, b) + (d(a2, b) + d(a3, b))


def _dot_exact_lhs(a01, b):
    a = a01.astype(BF16)
    b1 = b.astype(BF16)
    r1 = b - b1.astype(F32)
    b2 = r1.astype(BF16)
    b3 = (r1 - b2.astype(F32)).astype(BF16)
    d = functools.partial(jnp.dot, preferred_element_type=F32)
    return d(a, b1) + (d(a, b2) + d(a, b3))


def _sigmoid(x):
    return 1.0 / (1.0 + jnp.exp(-x))


def _softplus(x):
    return jnp.maximum(x, 0.0) + jnp.log(1.0 + jnp.exp(-jnp.abs(x)))


def _silu(x):
    return x * _sigmoid(x)


def _gelu_tanh(x):
    c = 0.7978845608028654
    return 0.5 * x * (1.0 + jnp.tanh(c * (x + 0.044715 * (x * x * x))))


def _rms_scale(x):
    return lax.rsqrt(jnp.mean(x * x, axis=-1, keepdims=True) + EPS)


def _params(sem):
    return pltpu.CompilerParams(dimension_semantics=sem, vmem_limit_bytes=VMEM_LIMIT_BYTES)


def _mixer_in_kernel(x_ref, g1_ref, wmain_ref, wgc_ref, wgr_ref,
                     lcw_ref, lcb_ref, wgate_ref, ba_ref, bx_ref, lam_ref,
                     gcw_ref, gpc_ref, gpr_ref,
                     lconv0_ref, lh0_ref, gconv0_ref,
                     outa_ref, q_ref, k_ref, v_ref, gz_ref, gcol_ref, grow_ref,
                     nlc_ref, nlh_ref, ngc_ref,
                     xl_buf, xg_buf, hcar, acum, bcum, *, tl):
    lt = pl.program_id(1)
    s1, s2, s3 = LRU_WIDTH, 2 * LRU_WIDTH, 2 * LRU_WIDTH + GDN_CONV_CH

    @pl.when(lt == 0)
    def _():
        xl_buf[0:SUBLANES, :] = lconv0_ref[0]
        xg_buf[0:SUBLANES, :] = gconv0_ref[0]
        hcar[...] = lh0_ref[0]

    @pl.when(lt > 0)
    def _():
        xl_buf[0:SUBLANES, :] = xl_buf[tl:tl + SUBLANES, :]
        xg_buf[0:SUBLANES, :] = xg_buf[tl:tl + SUBLANES, :]

    x = x_ref[0]
    hn = (x * _rms_scale(x) * g1_ref[...]).astype(BF16)

    xl_buf[SUBLANES:SUBLANES + tl, :] = jnp.dot(hn, wmain_ref[:, 0:s1], preferred_element_type=F32)
    base = SUBLANES - (CONV_W - 1)

    def causal_conv(buf, w_ref):
        xall = buf[...]
        acc = None
        for j in range(CONV_W):
            back = CONV_W - 1 - j
            shifted = pltpu.roll(xall, back, axis=0) if back else xall
            term = shifted[SUBLANES:, :] * w_ref[j:j + 1, :]
            acc = term if acc is None else acc + term
        return acc

    xc = causal_conv(xl_buf, lcw_ref) + lcb_ref[...]
    nlc_ref[0] = xl_buf[tl + base:tl + SUBLANES, :]

    half = LRU_WIDTH // 2
    pre = [_bdot(xc[:, s * half:(s + 1) * half], wgate_ref[s]) for s in range(2)]
    r_pre = jnp.concatenate([pre[0][:, :half], pre[1][:, :half]], axis=1)
    i_pre = jnp.concatenate([pre[0][:, half:], pre[1][:, half:]], axis=1)
    r = _sigmoid(r_pre + ba_ref[...])
    ig = _sigmoid(i_pre + bx_ref[...])
    log_a = (-LRU_C) * r * _softplus(-lam_ref[...])
    a = jnp.exp(log_a)
    b = jnp.sqrt(1.0 - jnp.exp(2.0 * log_a)) * (ig * xc)

    ng = tl // SUBLANES
    a3 = a.reshape(ng, SUBLANES, LRU_WIDTH)
    b3 = b.reshape(ng, SUBLANES, LRU_WIDTH)
    row = lax.broadcasted_iota(jnp.int32, (ng, SUBLANES, LRU_WIDTH), 1)
    sh = 1
    while sh < SUBLANES:
        a_s = pltpu.roll(a3, sh, axis=1)
        b_s = pltpu.roll(b3, sh, axis=1)
        m = row >= sh
        b3 = jnp.where(m, a3 * b_s + b3, b3)
        a3 = jnp.where(m, a3 * a_s, a3)
        sh *= 2
    h0 = hcar[...]
    if ng % SUBLANES:
        parts = []
        hlast = h0
        for g in range(ng):
            hg = a3[g] * hlast + b3[g]
            parts.append(hg)
            hlast = hg[SUBLANES - 1:SUBLANES, :]
        hs3 = jnp.stack(parts)
    else:
        def group_last(scr, val3):
            val = val3.reshape(tl, LRU_WIDTH)
            nch = LRU_WIDTH // LANES
            for ch in range(nch):
                scr[ch] = val[:, ch * LANES:(ch + 1) * LANES]
            return jnp.concatenate(
                [scr[ch, pl.ds(SUBLANES - 1, ng, stride=SUBLANES), :] for ch in range(nch)], axis=1)

        ag = group_last(acum, a3)
        bg = group_last(bcum, b3)
        grow_i = lax.broadcasted_iota(jnp.int32, (ng, LRU_WIDTH), 0)
        sh = 1
        while sh < ng:
            a_s = pltpu.roll(ag, sh, axis=0)
            b_s = pltpu.roll(bg, sh, axis=0)
            m = grow_i >= sh
            bg = jnp.where(m, ag * b_s + bg, bg)
            ag = jnp.where(m, ag * a_s, ag)
            sh *= 2
        hend = ag * h0 + bg
        hin = jnp.where(grow_i == 0, h0, pltpu.roll(hend, 1, axis=0))
        hs3 = a3 * hin.reshape(ng, 1, LRU_WIDTH) + b3
        hlast = hend[ng - 1:ng, :]
    hcar[...] = hlast
    nlh_ref[0] = hlast
    y = jnp.dot(hn, wmain_ref[:, s1:s2], preferred_element_type=F32)
    outa_ref[0] = (hs3.reshape(tl, LRU_WIDTH) * _gelu_tanh(y)).astype(BF16)

    xg_buf[SUBLANES:SUBLANES + tl, :] = jnp.dot(hn, wmain_ref[:, s2:s3], preferred_element_type=F32)
    ngc_ref[0] = xg_buf[tl + base:tl + SUBLANES, :]
    qkv = _silu(causal_conv(xg_buf, gcw_ref))
    for h in range(GDN_HEADS):
        qh = qkv[:, h * GDN_DK:(h + 1) * GDN_DK]
        kh = qkv[:, GDN_QK + h * GDN_DK:GDN_QK + (h + 1) * GDN_DK]
        qs = lax.rsqrt(jnp.sum(qh * qh, axis=-1, keepdims=True) + EPS) * (GDN_DK ** -0.5)
        ks = lax.rsqrt(jnp.sum(kh * kh, axis=-1, keepdims=True) + EPS)
        q_ref[0, :, h * GDN_DK:(h + 1) * GDN_DK] = qh * qs
        k_ref[0, :, h * GDN_DK:(h + 1) * GDN_DK] = kh * ks
    v_ref[0] = qkv[:, 2 * GDN_QK:]
    gz_ref[0] = _silu(jnp.dot(hn, wmain_ref[:, s3:], preferred_element_type=F32))

    gc = jnp.dot(hn, wgc_ref[...], preferred_element_type=F32)
    lane = lax.broadcasted_iota(jnp.int32, gc.shape, 1)
    gcol_ref[0] = jnp.where(lane < GDN_HEADS, _sigmoid(gc),
                            -gpc_ref[0:1, :] * _softplus(gc + gpc_ref[1:2, :]))
    gr = lax.dot_general(wgr_ref[...], hn, (((1,), (1,)), ((), ())), preferred_element_type=F32)
    sub = lax.broadcasted_iota(jnp.int32, gr.shape, 0)
    grow_ref[0] = jnp.where(sub < GDN_HEADS, _sigmoid(gr),
                            -gpr_ref[:, 0:1] * _softplus(gr + gpr_ref[:, 1:2]))


def _mixer_in(x, conv_l8, h_l, conv_g8, wts, tl):
    B, L, _ = x.shape
    nl = L // tl
    const2 = lambda b, l: (0, 0)
    const3 = lambda b, l: (0, 0, 0)
    per_b3 = lambda b, l: (b, 0, 0)
    tile3 = lambda b, l: (b, l, 0)
    in_specs = [
        pl.BlockSpec((1, tl, D_MODEL), tile3),
        pl.BlockSpec((1, D_MODEL), const2),
        pl.BlockSpec((D_MODEL, N_MAIN), const2),
        pl.BlockSpec((D_MODEL, N_GATE), const2),
        pl.BlockSpec((N_GATE, D_MODEL), const2),
        pl.BlockSpec((CONV_W, LRU_WIDTH), const2),
        pl.BlockSpec((1, LRU_WIDTH), const2),
        pl.BlockSpec((2, LRU_WIDTH // 2, LRU_WIDTH), const3),
        pl.BlockSpec((1, LRU_WIDTH), const2),
        pl.BlockSpec((1, LRU_WIDTH), const2),
        pl.BlockSpec((1, LRU_WIDTH), const2),
        pl.BlockSpec((CONV_W, GDN_CONV_CH), const2),
        pl.BlockSpec((2, N_GATE), const2),
        pl.BlockSpec((N_GATE, 2), const2),
        pl.BlockSpec((1, SUBLANES, LRU_WIDTH), per_b3),
        pl.BlockSpec((1, 1, LRU_WIDTH), per_b3),
        pl.BlockSpec((1, SUBLANES, GDN_CONV_CH), per_b3),
    ]
    out_shape = (
        jax.ShapeDtypeStruct((B, L, LRU_WIDTH), BF16),
        jax.ShapeDtypeStruct((B, L, GDN_QK), F32),
        jax.ShapeDtypeStruct((B, L, GDN_QK), F32),
        jax.ShapeDtypeStruct((B, L, GDN_V), F32),
        jax.ShapeDtypeStruct((B, L, GDN_V), F32),
        jax.ShapeDtypeStruct((B, L, N_GATE), F32),
        jax.ShapeDtypeStruct((B, N_GATE, L), F32),
        jax.ShapeDtypeStruct((B, CONV_W - 1, LRU_WIDTH), F32),
        jax.ShapeDtypeStruct((B, 1, LRU_WIDTH), F32),
        jax.ShapeDtypeStruct((B, CONV_W - 1, GDN_CONV_CH), F32),
    )
    out_specs = (
        pl.BlockSpec((1, tl, LRU_WIDTH), tile3),
        pl.BlockSpec((1, tl, GDN_QK), tile3),
        pl.BlockSpec((1, tl, GDN_QK), tile3),
        pl.BlockSpec((1, tl, GDN_V), tile3),
        pl.BlockSpec((1, tl, GDN_V), tile3),
        pl.BlockSpec((1, tl, N_GATE), tile3),
        pl.BlockSpec((1, N_GATE, tl), lambda b, l: (b, 0, l)),
        pl.BlockSpec((1, CONV_W - 1, LRU_WIDTH), per_b3),
        pl.BlockSpec((1, 1, LRU_WIDTH), per_b3),
        pl.BlockSpec((1, CONV_W - 1, GDN_CONV_CH), per_b3),
    )
    scratch = [
        pltpu.VMEM((tl + SUBLANES, LRU_WIDTH), F32),
        pltpu.VMEM((tl + SUBLANES, GDN_CONV_CH), F32),
        pltpu.VMEM((1, LRU_WIDTH), F32),
        pltpu.VMEM((LRU_WIDTH // LANES, tl, LANES), F32),
        pltpu.VMEM((LRU_WIDTH // LANES, tl, LANES), F32),
    ]
    return pl.pallas_call(
        functools.partial(_mixer_in_kernel, tl=tl),
        grid=(B, nl), in_specs=in_specs, out_specs=out_specs, out_shape=out_shape,
        scratch_shapes=scratch, compiler_params=_params(("arbitrary", "arbitrary")),
        name="mixer_in",
    )(x, wts["ln1_g"], wts["w_main"], wts["w_gate_col"], wts["w_gate_row"],
      wts["lru_cw"], wts["lru_cb"], wts["lru_wgate"], wts["lru_ba"], wts["lru_bx"], wts["lru_lam"],
      wts["gdn_cw"], wts["gate_par_col"], wts["gate_par_row"],
      conv_l8, h_l, conv_g8)


def _tri_inverse_all(lms, c):
    r = lax.broadcasted_iota(jnp.int32, (c, LANES), 0)
    lane = lax.broadcasted_iota(jnp.int32, (c, LANES), 1)
    col = jnp.bitwise_and(lane, c - 1)
    eye = (r == col).astype(F32)
    odd_block = jnp.bitwise_and(jnp.right_shift(lane, c.bit_length() - 1), 1) == 1

    def same_block(n):
        sh = n.bit_length() - 1
        return jnp.right_shift(r, sh) == jnp.right_shift(col, sh)

    def pieces(a, b):
        ah = a.astype(BF16)
        ah32 = ah.astype(F32)
        mixed = jnp.where(odd_block, a - ah32, ah32).astype(BF16)
        lhs = (mixed if 3 * c <= LANES else jnp.concatenate([mixed, ah], axis=1))[:, :3 * c]
        bh = b.astype(BF16)
        bl = (b - bh.astype(F32)).astype(BF16)
        return lhs, jnp.concatenate([bh, bh, bl], axis=0)

    def dot3_all(xs, ys):
        out = []
        for i in range(0, len(xs), 2):
            (l0, r0), (l1, r1) = pieces(xs[i], ys[i]), pieces(xs[i + 1], ys[i + 1])
            both = jnp.dot(jnp.concatenate([l0, l1], axis=0), jnp.concatenate([r0, r1], axis=1),
                           preferred_element_type=F32)
            out += [both[:c, :LANES], both[c:, LANES:]]
        return out

    blk8 = same_block(SUBLANES)
    l8 = [jnp.where(blk8, lm, 0.0) for lm in lms]
    l2 = dot3_all(l8, l8)
    n1 = [eye - a for a in l8]
    p = [a + b for a, b in zip(n1, dot3_all(n1, l2))]
    l4 = dot3_all(l2, l2)
    p = [a + b for a, b in zip(p, dot3_all(p, l4))]
    n = SUBLANES
    while n < c:
        off = same_block(2 * n) & jnp.logical_not(same_block(n))
        lo = [jnp.where(off, lm, 0.0) for lm in lms]
        t = dot3_all(p, lo)
        p = [a - b for a, b in zip(p, dot3_all(t, p))]
        n *= 2
    return p


def _gdn_kernel(q_ref, k_ref, v_ref, gz_ref, gcol_ref, grow_ref, s0_ref, ng_ref,
                o_ref, sout_ref, s_scr, *, bb, c, cps):
    ct = pl.program_id(1)

    @pl.when(ct == 0)
    def _():
        s_scr[...] = s0_ref[...]

    reps = LANES // c
    r = lax.broadcasted_iota(jnp.int32, (c, LANES), 0)
    col = jnp.bitwise_and(lax.broadcasted_iota(jnp.int32, (c, LANES), 1), c - 1)
    lower = r >= col
    strict = r > col
    rs = lax.broadcasted_iota(jnp.int32, (c, c), 0)
    cs = lax.broadcasted_iota(jnp.int32, (c, c), 1)
    tri_l = (rs >= cs).astype(F32)
    tri_u = jnp.where(r <= col, 1.0, 0.0)

    def hs(h):
        return slice(h * GDN_DK, (h + 1) * GDN_DK)

    def rows(j):
        return slice(j * c, (j + 1) * c)

    chains = [(j, b, h) for j in range(cps) for b in range(bb) for h in range(GDN_HEADS)]
    pairs = [(j, b) for j in range(cps) for b in range(bb)]
    gates_c = {jb: gcol_ref[jb[1], rows(jb[0]), :] for jb in pairs}
    gcum_c = {jb: _dot_exact_lhs(tri_l, gates_c[jb]) for jb in pairs}
    gcum_r = {jb: _dot_exact_rhs(grow_ref[jb[1], jb[0]], tri_u) for jb in pairs}
    beta = [gates_c[(j, b)][:, h:h + 1] for j, b, h in chains]
    gc_c = [gcum_c[(j, b)][:, GDN_HEADS + h:GDN_HEADS + h + 1] for j, b, h in chains]
    gc_r = [gcum_r[(j, b)][GDN_HEADS + h:GDN_HEADS + h + 1, :] for j, b, h in chains]
    decay = [jnp.where(lower, jnp.exp(jnp.where(lower, a - b, 0.0)), 0.0) for a, b in zip(gc_c, gc_r)]
    e_c = [jnp.exp(a) for a in gc_c]
    kb = [k_ref[b, rows(j), hs(h)] * bt for (j, b, h), bt in zip(chains, beta)]
    kk = [_bdot_nt(jnp.concatenate([a, q_ref[b, rows(j), hs(h)]], axis=0),
                   jnp.concatenate([k_ref[b, rows(j), hs(h)]] * reps, axis=0))
          for (j, b, h), a in zip(chains, kb)]
    lmat = [jnp.where(strict, x[:c] * d, 0.0) for x, d in zip(kk, decay)]
    attn = [x[c:, :c] * d[:, :c] for x, d in zip(kk, decay)]
    t_inv = _tri_inverse_all(lmat, c)
    uw = [_bdot(t[:, :c], jnp.concatenate([v_ref[b, rows(j), hs(h)] * bt, a * e], axis=1))
          for (j, b, h), t, bt, a, e in zip(chains, t_inv, beta, kb, e_c)]
    per = bb * GDN_HEADS
    for j in range(cps):
        sel = range(j * per, (j + 1) * per)
        ws = [_bdot(jnp.concatenate([uw[i][:, GDN_DV:],
                                     q_ref[chains[i][1], rows(j), hs(chains[i][2])] * e_c[i]], axis=0),
                    s_scr[chains[i][1], chains[i][2]]) for i in sel]
        v_new = [uw[i][:, :GDN_DV] - x[:c] for i, x in zip(sel, ws)]
        o = [x[c:] + _bdot(attn[i], vn) for i, x, vn in zip(sel, ws, v_new)]
        for i, vn in zip(sel, v_new):
            _, b, h = chains[i]
            g_last = gc_c[i][c - 1:c, :]
            k_dec = k_ref[b, rows(j), hs(h)] * jnp.exp(g_last - gc_c[i])
            s_scr[b, h] = s_scr[b, h] * jnp.exp(g_last) + _bdot_tn(k_dec, vn)
        for i, x in zip(sel, o):
            _, b, h = chains[i]
            on = x * _rms_scale(x) * ng_ref[...]
            o_ref[b, rows(j), hs(h)] = (on * gz_ref[b, rows(j), hs(h)]).astype(BF16)

    sout_ref[...] = s_scr[...]


def _gdn(q, k, v, gz, gcol, grow, s0, ng, bb, c, cps):
    B, L, _ = q.shape
    nc = L // c
    tl = cps * c
    tile = lambda i, t: (i, t, 0)
    per_b4 = lambda i, t: (i, 0, 0, 0)
    in_specs = [
        pl.BlockSpec((bb, tl, GDN_QK), tile),
        pl.BlockSpec((bb, tl, GDN_QK), tile),
        pl.BlockSpec((bb, tl, GDN_V), tile),
        pl.BlockSpec((bb, tl, GDN_V), tile),
        pl.BlockSpec((bb, tl, N_GATE), tile),
        pl.BlockSpec((bb, cps, N_GATE, c), lambda i, t: (i, t, 0, 0)),
        pl.BlockSpec((bb, GDN_HEADS, GDN_DK, GDN_DV), per_b4),
        pl.BlockSpec((1, GDN_DV), lambda i, t: (0, 0)),
    ]
    out_shape = (jax.ShapeDtypeStruct((B, L, GDN_V), BF16),
                 jax.ShapeDtypeStruct((B, GDN_HEADS, GDN_DK, GDN_DV), F32))
    out_specs = (pl.BlockSpec((bb, tl, GDN_V), tile),
                 pl.BlockSpec((bb, GDN_HEADS, GDN_DK, GDN_DV), per_b4))
    return pl.pallas_call(
        functools.partial(_gdn_kernel, bb=bb, c=c, cps=cps),
        grid=(B // bb, nc // cps), in_specs=in_specs, out_specs=out_specs, out_shape=out_shape,
        scratch_shapes=[pltpu.VMEM((bb, GDN_HEADS, GDN_DK, GDN_DV), F32)],
        compiler_params=_params(("arbitrary", "arbitrary")),
        name="gdn",
    )(q, k, v, gz, gcol, grow.reshape(B, N_GATE, nc, c).transpose(0, 2, 1, 3), s0, ng)


def _mixer_out_kernel(oa_ref, og_ref, x_ref, wout_ref, g2_ref, wr_ref, br_ref,
                      x1_ref, h2p_ref, ew_ref, ei_ref, cnt_ref, carry, *, tl):
    @pl.when(pl.program_id(0) == 0)
    def _():
        carry[...] = jnp.zeros_like(carry)

    mix = (jnp.dot(oa_ref[...], wout_ref[0:LRU_WIDTH, :], preferred_element_type=F32)
           + jnp.dot(og_ref[...], wout_ref[LRU_WIDTH:, :], preferred_element_type=F32))
    x1 = x_ref[...] + mix
    x1_ref[...] = x1
    h2 = x1 * _rms_scale(x1) * g2_ref[...]
    bits = pltpu.bitcast(h2.astype(BF16).astype(F32), jnp.uint32)
    h2p_ref[...] = (bits[:, :D_MODEL // 2] & jnp.uint32(0xFFFF0000)) | (bits[:, D_MODEL // 2:] >> 16)
    logits = _bdot(h2, wr_ref[...]) + br_ref[...]
    lane = lax.broadcasted_iota(jnp.int32, logits.shape, 1)
    lane_f = lane.astype(F32)
    neg = jnp.float32(-jnp.inf)
    big = jnp.float32(1e9)
    gmask = lane < N_GROUPS
    gl = jnp.where(gmask, logits, neg)
    gmax = jnp.max(gl, axis=1, keepdims=True)
    grp = jnp.min(jnp.where(gl == gmax, lane_f, big), axis=1, keepdims=True)
    p_grp = 1.0 / jnp.sum(jnp.where(gmask, jnp.exp(gl - gmax), 0.0), axis=1, keepdims=True)
    egrp = jnp.right_shift(lane - N_GROUPS, EXPERTS_PER_GROUP.bit_length() - 1).astype(F32)
    sel = (lane >= N_GROUPS) & (lane < N_GROUPS + N_EXPERTS) & (egrp == grp)
    el = jnp.where(sel, logits, neg)
    v1 = jnp.max(el, axis=1, keepdims=True)
    i1 = jnp.min(jnp.where(el == v1, lane_f, big), axis=1, keepdims=True)
    el2 = jnp.where(lane_f == i1, neg, el)
    v2 = jnp.max(el2, axis=1, keepdims=True)
    i2 = jnp.min(jnp.where(el2 == v2, lane_f, big), axis=1, keepdims=True)
    e2 = jnp.exp(v2 - v1)
    den = 1.0 / (1.0 + e2)
    w1 = den * p_grp
    w2 = (e2 * den) * p_grp
    ew_ref[...] = jnp.where(lane == 0, w1, jnp.where(lane == 1, w2, 0.0))

    ex1 = i1 - N_GROUPS
    ex2 = i2 - N_GROUPS
    hit1 = lane_f == ex1
    hit2 = lane_f == ex2
    oh = jnp.concatenate([jnp.where(hit1, 1.0, 0.0), jnp.where(hit2, 1.0, 0.0)], axis=1)
    rr = lax.broadcasted_iota(jnp.int32, (tl, tl), 0)
    cc = lax.broadcasted_iota(jnp.int32, (tl, tl), 1)
    before = jnp.dot(jnp.where(rr > cc, 1.0, 0.0).astype(BF16), oh.astype(BF16),
                     preferred_element_type=F32)
    tot = jnp.sum(oh, axis=0, keepdims=True)
    tot1, tot2 = tot[:, :LANES], tot[:, LANES:]
    c0 = carry[...]
    rank1 = jnp.sum(jnp.where(hit1, c0 + before[:, :LANES], 0.0), axis=1, keepdims=True)
    rank2 = jnp.sum(jnp.where(hit2, (c0 + tot1) + before[:, LANES:], 0.0), axis=1, keepdims=True)
    c1 = c0 + (tot1 + tot2)
    carry[...] = c1
    cnt_ref[...] = c1
    idx = jnp.where(lane == 0, ex1, jnp.where(lane == 1, ex2, jnp.where(
        lane == 2, rank1, jnp.where(lane == 3, rank2, 0.0))))
    ei_ref[...] = jnp.transpose(idx)[0:SUBLANES, :].astype(jnp.int32)


def _mixer_out(oa, og, x, wts, tl):
    T = x.shape[0]
    tile = lambda i: (i, 0)
    const = lambda i: (0, 0)
    in_specs = [
        pl.BlockSpec((tl, LRU_WIDTH), tile),
        pl.BlockSpec((tl, GDN_V), tile),
        pl.BlockSpec((tl, D_MODEL), tile),
        pl.BlockSpec((D_MODEL, D_MODEL), const),
        pl.BlockSpec((1, D_MODEL), const),
        pl.BlockSpec((D_MODEL, LANES), const),
        pl.BlockSpec((1, LANES), const),
    ]
    out_shape = (jax.ShapeDtypeStruct((T, D_MODEL), F32),
                 jax.ShapeDtypeStruct((T, D_MODEL // 2), jnp.uint32),
                 jax.ShapeDtypeStruct((T, LANES), F32),
                 jax.ShapeDtypeStruct((SUBLANES, T), jnp.int32),
                 jax.ShapeDtypeStruct((1, LANES), F32))
    out_specs = (pl.BlockSpec((tl, D_MODEL), tile), pl.BlockSpec((tl, D_MODEL // 2), tile),
                 pl.BlockSpec((tl, LANES), tile), pl.BlockSpec((SUBLANES, tl), lambda i: (0, i)),
                 pl.BlockSpec((1, LANES), const))
    return pl.pallas_call(
        functools.partial(_mixer_out_kernel, tl=tl), grid=(T // tl,), in_specs=in_specs,
        out_specs=out_specs, out_shape=out_shape, scratch_shapes=[pltpu.VMEM((1, LANES), F32)],
        compiler_params=_params(("arbitrary",)), name="mixer_out",
    )(oa, og, x, wts["w_out"], wts["ln2_g"], wts["w_router"], wts["b_router"])


def _row_copy(src, src_row, dst, dst_row, sem):
    return pltpu.make_async_copy(src.at[pl.ds(src_row, 1), :], dst.at[pl.ds(dst_row, 1), :], sem)


def _dispatch_kernel(pe_ref, dest_ref, h2p_ref, xs_hbm, zbuf, sem_z, sem_r, *, td, rb, n_blocks):
    i = pl.program_id(0)

    def tail_copy(e):
        end = pe_ref[e]
        start = pl.multiple_of(end - rb, rb)
        return pltpu.make_async_copy(zbuf, xs_hbm.at[pl.ds(start, rb), :], sem_z)

    def nonempty(e):
        return pe_ref[e] > (pe_ref[e - 1] if e > 0 else 0)

    def unused_copy(j):
        return pltpu.make_async_copy(zbuf, xs_hbm.at[pl.ds(pl.multiple_of(j * rb, rb), rb), :], sem_z)

    @pl.when(i == 0)
    def _():
        zbuf[...] = jnp.zeros_like(zbuf)
        for e in range(N_EXPERTS):
            pl.when(nonempty(e))(lambda e=e: tail_copy(e).start())
        for e in range(N_EXPERTS):
            pl.when(nonempty(e))(lambda e=e: tail_copy(e).wait())
        first_unused = lax.shift_right_logical(pe_ref[N_EXPERTS - 1], rb.bit_length() - 1)
        lax.fori_loop(first_unused, n_blocks, lambda j, c: (unused_copy(j).start(), c)[1], 0)
        lax.fori_loop(first_unused, n_blocks, lambda j, c: (unused_copy(j).wait(), c)[1], 0)

    def copies(r):
        return [_row_copy(h2p_ref, r, xs_hbm, dest_ref[0, 0, k * td + r], sem_r)
                for k in range(TOP_K)]

    for r in range(td):
        for k, cp in enumerate(copies(r)):
            cp.start(priority=k)
    for r in range(td):
        for cp in copies(r):
            cp.wait()


def _dispatch(pad_end, dest_tiles, h2p, n_rows, td, rb):
    T = h2p.shape[0]
    grid_spec = pltpu.PrefetchScalarGridSpec(
        num_scalar_prefetch=1, grid=(T // td,),
        in_specs=[pl.BlockSpec((1, 1, TOP_K * td), lambda i, pe: (i, 0, 0), memory_space=pltpu.SMEM),
                  pl.BlockSpec((td, D_MODEL // 2), lambda i, pe: (i, 0))],
        out_specs=pl.BlockSpec(memory_space=pl.ANY),
        scratch_shapes=[pltpu.VMEM((rb, D_MODEL // 2), jnp.uint32),
                        pltpu.SemaphoreType.DMA(()), pltpu.SemaphoreType.DMA(())],
    )
    return pl.pallas_call(
        functools.partial(_dispatch_kernel, td=td, rb=rb, n_blocks=n_rows // rb), grid_spec=grid_spec,
        out_shape=jax.ShapeDtypeStruct((n_rows, D_MODEL // 2), jnp.uint32),
        compiler_params=_params(("arbitrary",)), name="dispatch",
    )(pad_end, dest_tiles, h2p)


def _moe_kernel(be_ref, nu_ref, xs_ref, w1_ref, w3_ref, w2_ref, yb_ref):
    del be_ref
    half = D_MODEL // 2

    @pl.when(pl.program_id(0) >= nu_ref[0])
    def _():
        yb_ref[...] = jnp.zeros_like(yb_ref)

    @pl.when(pl.program_id(0) < nu_ref[0])
    def _():
        u = xs_ref[...]
        xa = pltpu.bitcast(u & jnp.uint32(0xFFFF0000), F32).astype(BF16)
        xb = pltpu.bitcast(u << 16, F32).astype(BF16)

        def proj(w_ref):
            return (jnp.dot(xa, w_ref[0, 0:half, :].astype(BF16), preferred_element_type=F32)
                    + jnp.dot(xb, w_ref[0, half:, :].astype(BF16), preferred_element_type=F32))

        hmid = _silu(proj(w1_ref)) * proj(w3_ref)
        yb_ref[...] = _bdot(hmid, w2_ref[0])


def _moe(block_e, n_used, xs, wts, rb):
    n_blocks = block_e.shape[0]

    grid_spec = pltpu.PrefetchScalarGridSpec(
        num_scalar_prefetch=2, grid=(n_blocks,),
        in_specs=[
            pl.BlockSpec((rb, D_MODEL // 2), lambda i, be, nu: (i, 0)),
            pl.BlockSpec((1, D_MODEL, D_EXPERT), lambda i, be, nu: (be[i], 0, 0)),
            pl.BlockSpec((1, D_MODEL, D_EXPERT), lambda i, be, nu: (be[i], 0, 0)),
            pl.BlockSpec((1, D_EXPERT, D_MODEL), lambda i, be, nu: (be[i], 0, 0)),
        ],
        out_specs=pl.BlockSpec((rb, D_MODEL), lambda i, be, nu: (i, 0)),
    )
    return pl.pallas_call(
        _moe_kernel, grid_spec=grid_spec,
        out_shape=jax.ShapeDtypeStruct((n_blocks * rb, D_MODEL), F32),
        compiler_params=_params(("arbitrary",)), name="moe",
    )(block_e, n_used, xs, wts["moe_w1"], wts["moe_w3"], wts["moe_w2"])


def _dispatch_plan(ei, cnt, rb, td, tc):
    T = ei.shape[1]
    n_blocks = -(-(T * TOP_K + N_EXPERTS * (rb - 1)) // rb)
    counts = cnt[0, :N_EXPERTS].astype(jnp.int32)
    padded = (counts + rb - 1) // rb * rb
    pad_end = jnp.cumsum(padded).astype(jnp.int32)
    pad_start = pad_end - padded
    e = ei[0:TOP_K, :]
    rank = ei[TOP_K:2 * TOP_K, :]
    dest = rank
    for j in range(N_EXPERTS):
        dest = dest + jnp.where(e == j, pad_start[j], 0)

    def tiles(t):
        return dest.reshape(TOP_K, T // t, t).transpose(1, 0, 2).reshape(T // t, 1, TOP_K * t)

    blk_start = jnp.arange(n_blocks, dtype=jnp.int32) * rb
    block_e = jnp.minimum(jnp.sum(pad_end[None, :] <= blk_start[:, None], axis=1),
                          N_EXPERTS - 1).astype(jnp.int32)
    n_used = (pad_end[N_EXPERTS - 1:] // rb).astype(jnp.int32)
    return pad_end, tiles(td), tiles(tc), block_e, n_used, n_blocks * rb


def _combine_kernel(dcur_ref, dnext_ref, x1_ref, ew_ref, gf_ref, yb_hbm, y_ref, ybuf, sems,
                    *, tc, n_tiles):
    i = pl.program_id(0)

    def copies(dref, slot, r):
        return [pltpu.make_async_copy(yb_hbm.at[pl.ds(dref[0, 0, k * tc + r], 1), :],
                                      ybuf.at[slot, k, pl.ds(r, 1), :], sems.at[slot])
                for k in range(TOP_K)]

    def gather(dref, slot):
        for r in range(tc):
            for k, cp in enumerate(copies(dref, slot, r)):
                cp.start(priority=k)

    def drain(dref, slot):
        for r in range(tc):
            for cp in copies(dref, slot, r):
                cp.wait()

    pl.when(i == 0)(lambda: gather(dcur_ref, 0))

    def step(slot):
        drain(dcur_ref, slot)
        gather(dnext_ref, 1 - slot)
        w = ew_ref[...]
        moe = w[:, 0:1] * ybuf[slot, 0] + w[:, 1:2] * ybuf[slot, 1]
        x2 = x1_ref[...] + moe
        y_ref[...] = x2 * _rms_scale(x2) * gf_ref[...]
        pl.when(i == n_tiles - 1)(lambda: drain(dnext_ref, 1 - slot))

    pl.when(lax.rem(i, 2) == 0)(lambda: step(0))
    pl.when(lax.rem(i, 2) == 1)(lambda: step(1))


def _combine(dest_tiles, x1, ew, final_g, yb, tc):
    T = x1.shape[0]
    n_tiles = T // tc
    tile = lambda i: (i, 0)
    dspec = lambda f: pl.BlockSpec((1, 1, TOP_K * tc), f, memory_space=pltpu.SMEM)
    return pl.pallas_call(
        functools.partial(_combine_kernel, tc=tc, n_tiles=n_tiles), grid=(n_tiles,),
        in_specs=[dspec(lambda i: (i, 0, 0)),
                  dspec(lambda i: (jnp.minimum(i + 1, n_tiles - 1), 0, 0)),
                  pl.BlockSpec((tc, D_MODEL), tile), pl.BlockSpec((tc, LANES), tile),
                  pl.BlockSpec((1, D_MODEL), lambda i: (0, 0)),
                  pl.BlockSpec(memory_space=pl.ANY)],
        out_specs=pl.BlockSpec((tc, D_MODEL), tile),
        out_shape=jax.ShapeDtypeStruct((T, D_MODEL), F32),
        scratch_shapes=[pltpu.VMEM((2, TOP_K, tc, D_MODEL), F32), pltpu.SemaphoreType.DMA((2,))],
        compiler_params=_params(("arbitrary",)), name="combine",
    )(dest_tiles, dest_tiles, x1, ew, final_g, yb)


def _prep_weights(ln1_g, ln2_g, w_in, lru_conv_w, lru_conv_b, lru_wa, lru_ba, lru_wx, lru_bx,
                  lru_lambda, gdn_conv_w, gdn_a_log, gdn_dt_bias, gdn_norm_g, w_out,
                  router_group_w, router_group_b, router_expert_w, router_expert_b,
                  moe_w1, moe_w3, moe_w2, final_g):
    w_in0 = w_in[0]
    half = LRU_WIDTH // 2
    per_half = half // LRU_BLOCK

    def block_diag(w):
        w4 = w.reshape(2, per_half, LRU_BLOCK, LRU_BLOCK)
        eye = jnp.eye(per_half, dtype=w.dtype)
        return jnp.einsum("snij,nm->snimj", w4, eye).reshape(2, half, half)

    wgate = jnp.concatenate([block_diag(lru_wa[0]), block_diag(lru_wx[0])], axis=2).astype(BF16)
    zeros4 = jnp.zeros((GDN_HEADS,), F32)
    neg_a = jnp.concatenate([zeros4, jnp.exp(gdn_a_log[0].astype(F32))])
    dtb = jnp.concatenate([zeros4, gdn_dt_bias[0].astype(F32)])
    gate_par = jnp.stack([neg_a, dtb])
    w_router = jnp.zeros((D_MODEL, LANES), F32)
    w_router = w_router.at[:, :N_GROUPS].set(router_group_w[0])
    w_router = w_router.at[:, N_GROUPS:N_GROUPS + N_EXPERTS].set(router_expert_w[0])
    b_router = jnp.zeros((1, LANES), F32)
    b_router = b_router.at[0, :N_GROUPS].set(router_group_b[0])
    b_router = b_router.at[0, N_GROUPS:N_GROUPS + N_EXPERTS].set(router_expert_b[0])
    return {
        "ln1_g": ln1_g[0].reshape(1, D_MODEL), "ln2_g": ln2_g[0].reshape(1, D_MODEL),
        "w_main": w_in0.astype(BF16),
        "w_gate_col": w_in0[:, N_MAIN:].astype(BF16),
        "w_gate_row": w_in0[:, N_MAIN:].T.astype(BF16),
        "lru_cw": lru_conv_w[0], "lru_cb": lru_conv_b[0].reshape(1, LRU_WIDTH),
        "lru_wgate": wgate,
        "lru_ba": lru_ba[0].reshape(1, LRU_WIDTH), "lru_bx": lru_bx[0].reshape(1, LRU_WIDTH),
        "lru_lam": lru_lambda[0].reshape(1, LRU_WIDTH),
        "gdn_cw": gdn_conv_w[0], "gate_par_col": gate_par, "gate_par_row": gate_par.T,
        "gdn_ng": gdn_norm_g[0].reshape(1, GDN_DV),
        "w_out": w_out[0].astype(BF16),
        "w_router": w_router, "b_router": b_router,
        "moe_w1": moe_w1[0], "moe_w3": moe_w3[0], "moe_w2": moe_w2[0],
        "final_g": final_g.reshape(1, D_MODEL),
    }


def _pad_conv_state(s):
    return jnp.pad(s, ((0, 0), (SUBLANES - (CONV_W - 1), 0), (0, 0)))


def _trunk(x, conv_l, h_l, conv_g, s_g, wts, tl_in, bb, tl_tok, rb, td):
    B, L, _ = x.shape
    T = B * L
    c = min(CHUNK, L)
    oa, q, k, v, gz, gcol, grow, nlc, nlh, ngc = _mixer_in(
        x, _pad_conv_state(conv_l), h_l.reshape(B, 1, LRU_WIDTH), _pad_conv_state(conv_g), wts, tl_in)
    og, ns = _gdn(q, k, v, gz, gcol, grow, s_g, wts["gdn_ng"], bb, c, min(2, L // c))
    x1, h2p, ew, ei, cnt = _mixer_out(oa.reshape(T, LRU_WIDTH), og.reshape(T, GDN_V),
                                      x.reshape(T, D_MODEL), wts, tl_tok)
    pad_end, dest_d, dest_c, block_e, n_used, n_rows = _dispatch_plan(ei, cnt, rb, td, tl_tok)
    xs = _dispatch(pad_end, dest_d, h2p, n_rows, td, rb)
    yb = _moe(block_e, n_used, xs, wts, rb)
    y = _combine(dest_c, x1, ew, wts["final_g"], yb, tl_tok)
    return (y.reshape(B, L, D_MODEL), nlc[None], nlh.reshape(1, B, LRU_WIDTH), ngc[None], ns[None])


def kernel(x_prompt, x_sample, state_lru_conv, state_lru_h, state_gdn_conv, state_gdn_S, ln1_g, ln2_g, w_in, lru_conv_w, lru_conv_b, lru_wa, lru_ba, lru_wx, lru_bx, lru_lambda, gdn_conv_w, gdn_a_log, gdn_dt_bias, gdn_norm_g, w_out, router_group_w, router_group_b, router_expert_w, router_expert_b, moe_w1, moe_w3, moe_w2, final_g):
    wts = _prep_weights(ln1_g, ln2_g, w_in, lru_conv_w, lru_conv_b, lru_wa, lru_ba, lru_wx, lru_bx,
                        lru_lambda, gdn_conv_w, gdn_a_log, gdn_dt_bias, gdn_norm_g, w_out,
                        router_group_w, router_group_b, router_expert_w, router_expert_b,
                        moe_w1, moe_w3, moe_w2, final_g)
    B, L, _ = x_prompt.shape
    dt = x_prompt.dtype
    y_p, p_lc, p_lh, p_gc, p_gs = _trunk(
        x_prompt,
        jnp.zeros((B, CONV_W - 1, LRU_WIDTH), dt), jnp.zeros((B, LRU_WIDTH), dt),
        jnp.zeros((B, CONV_W - 1, GDN_CONV_CH), dt), jnp.zeros((B, GDN_HEADS, GDN_DK, GDN_DV), dt),
        wts, tl_in=512, bb=4, tl_tok=512, rb=512, td=512)
    Bs, Ls, _ = x_sample.shape
    y_s, s_lc, s_lh, s_gc, s_gs = _trunk(
        x_sample, state_lru_conv[0], state_lru_h[0], state_gdn_conv[0], state_gdn_S[0],
        wts, tl_in=Ls, bb=2, tl_tok=Bs * Ls, rb=32, td=Bs * Ls)
    return (y_p, y_s, p_lc, p_lh, p_gc, p_gs, s_lc, s_lh, s_gc, s_gs)
```

```python
import functools

import jax
import jax.numpy as jnp
from jax import lax
from jax.experimental import pallas as pl
from jax.experimental.pallas import tpu as pltpu

F32 = jnp.float32
BF16 = jnp.bfloat16

D_MODEL = 1024
CONV_W = 4
LRU_WIDTH = 512
LRU_BLOCKS = 8
LRU_BLOCK = LRU_WIDTH // LRU_BLOCKS
LRU_C = 8.0
GDN_HEADS = 4
GDN_DK = 128
GDN_DV = 128
GDN_QK = GDN_HEADS * GDN_DK
GDN_V = GDN_HEADS * GDN_DV
GDN_CONV_CH = 2 * GDN_QK + GDN_V
N_GATE = 2 * GDN_HEADS
N_MAIN = 2 * LRU_WIDTH + GDN_CONV_CH + GDN_V
CHUNK = 64
N_GROUPS = 4
EXPERTS_PER_GROUP = 8
N_EXPERTS = N_GROUPS * EXPERTS_PER_GROUP
TOP_K = 2
D_EXPERT = 256
EPS = 1e-6

SUBLANES = 8
LANES = 128
MXU_DIM = 256
VMEM_LIMIT_BYTES = 48 * 1024 * 1024


def _bdot(a, b):
    return jnp.dot(a.astype(BF16), b.astype(BF16), preferred_element_type=F32)


def _bdot_nt(a, b):
    return lax.dot_general(a.astype(BF16), b.astype(BF16), (((1,), (1,)), ((), ())),
                           preferred_element_type=F32)


def _bdot_tn(a, b):
    return lax.dot_general(a.astype(BF16), b.astype(BF16), (((0,), (0,)), ((), ())),
                           preferred_element_type=F32)


def _split2(a):
    hi = a.astype(BF16)
    lo = (a - hi.astype(F32)).astype(BF16)
    return hi, lo


def _dot3(a, b):
    ah, al = _split2(a)
    bh, bl = _split2(b)
    d = functools.partial(jnp.dot, preferred_element_type=F32)
    if 3 * a.shape[1] <= MXU_DIM:
        return d(jnp.concatenate([ah, ah, al], axis=1), jnp.concatenate([bh, bl, bh], axis=0))
    return d(ah, bh) + (d(ah, bl) + d(al, bh))


def _dot_exact_rhs(a, b01):
    b = b01.astype(BF16)
    a1 = a.astype(BF16)
    r1 = a - a1.astype(F32)
    a2 = r1.astype(BF16)
    a3 = (r1 - a2.astype(F32)).astype(BF16)
    d = functools.partial(jnp.dot, preferred_element_type=F32)
    return d(a1, b) + (d(a2, b) + d(a3, b))


def _dot_exact_lhs(a01, b):
    a = a01.astype(BF16)
    b1 = b.astype(BF16)
    r1 = b - b1.astype(F32)
    b2 = r1.astype(BF16)
    b3 = (r1 - b2.astype(F32)).astype(BF16)
    d = functools.partial(jnp.dot, preferred_element_type=F32)
    return d(a, b1) + (d(a, b2) + d(a, b3))


NEG_LOG2_E = -1.4426950408889634


def _sigmoid(x):
    return 1.0 / (1.0 + jnp.exp2(x * NEG_LOG2_E))


def _softplus(x):
    return jnp.maximum(x, 0.0) + jnp.log(1.0 + jnp.exp(-jnp.abs(x)))


def _silu(x):
    return x * _sigmoid(x)


def _gelu_tanh(x):
    c = 0.7978845608028654
    return 0.5 * x * (1.0 + jnp.tanh(c * (x + 0.044715 * (x * x * x))))


def _rms_scale(x):
    return lax.rsqrt(jnp.mean(x * x, axis=-1, keepdims=True) + EPS)


def _params(sem):
    return pltpu.CompilerParams(dimension_semantics=sem, vmem_limit_bytes=VMEM_LIMIT_BYTES)


def _mixer_in_kernel(x_ref, g1_ref, wmain_ref, wgc_ref, wgr_ref,
                     lcw_ref, lcb_ref, wgate_ref, ba_ref, bx_ref, lam_ref,
                     gcw_ref, gpc_ref, gpr_ref,
                     lconv0_ref, lh0_ref, gconv0_ref,
                     outa_ref, q_ref, k_ref, v_ref, gz_ref, gcol_ref, grow_ref,
                     nlc_ref, nlh_ref, ngc_ref,
                     xl_buf, xg_buf, hcar, acum, bcum, *, tl):
    lt = pl.program_id(1)
    s1, s2, s3 = LRU_WIDTH, 2 * LRU_WIDTH, 2 * LRU_WIDTH + GDN_CONV_CH

    @pl.when(lt == 0)
    def _():
        xl_buf[0:SUBLANES, :] = lconv0_ref[0]
        xg_buf[0:SUBLANES, :] = gconv0_ref[0]
        hcar[...] = lh0_ref[0]

    @pl.when(lt > 0)
    def _():
        xl_buf[0:SUBLANES, :] = xl_buf[tl:tl + SUBLANES, :]
        xg_buf[0:SUBLANES, :] = xg_buf[tl:tl + SUBLANES, :]

    x = x_ref[0]
    hn = (x * _rms_scale(x) * g1_ref[...]).astype(BF16)

    xl_buf[SUBLANES:SUBLANES + tl, :] = jnp.dot(hn, wmain_ref[:, 0:s1], preferred_element_type=F32)
    base = SUBLANES - (CONV_W - 1)

    def causal_conv(buf, w_ref):
        xall = buf[...]
        acc = None
        for j in range(CONV_W):
            back = CONV_W - 1 - j
            shifted = pltpu.roll(xall, back, axis=0) if back else xall
            term = shifted[SUBLANES:, :] * w_ref[j:j + 1, :]
            acc = term if acc is None else acc + term
        return acc

    xc = causal_conv(xl_buf, lcw_ref) + lcb_ref[...]
    nlc_ref[0] = xl_buf[tl + base:tl + SUBLANES, :]

    half = LRU_WIDTH // 2
    pre = [_bdot(xc[:, s * half:(s + 1) * half], wgate_ref[s]) for s in range(2)]
    r_pre = jnp.concatenate([pre[0][:, :half], pre[1][:, :half]], axis=1)
    i_pre = jnp.concatenate([pre[0][:, half:], pre[1][:, half:]], axis=1)
    r = _sigmoid(r_pre + ba_ref[...])
    ig = _sigmoid(i_pre + bx_ref[...])
    log_a = (-LRU_C) * r * _softplus(-lam_ref[...])
    a = jnp.exp(log_a)
    b = jnp.sqrt(1.0 - jnp.exp(2.0 * log_a)) * (ig * xc)

    ng = tl // SUBLANES
    a3 = a.reshape(ng, SUBLANES, LRU_WIDTH)
    b3 = b.reshape(ng, SUBLANES, LRU_WIDTH)
    row = lax.broadcasted_iota(jnp.int32, (ng, SUBLANES, LRU_WIDTH), 1)
    sh = 1
    while sh < SUBLANES:
        a_s = pltpu.roll(a3, sh, axis=1)
        b_s = pltpu.roll(b3, sh, axis=1)
        m = row >= sh
        b3 = jnp.where(m, a3 * b_s + b3, b3)
        a3 = jnp.where(m, a3 * a_s, a3)
        sh *= 2
    h0 = hcar[...]
    if ng % SUBLANES:
        parts = []
        hlast = h0
        for g in range(ng):
            hg = a3[g] * hlast + b3[g]
            parts.append(hg)
            hlast = hg[SUBLANES - 1:SUBLANES, :]
        hs3 = jnp.stack(parts)
    else:
        def group_last(scr, val3):
            val = val3.reshape(tl, LRU_WIDTH)
            nch = LRU_WIDTH // LANES
            for ch in range(nch):
                scr[ch] = val[:, ch * LANES:(ch + 1) * LANES]
            return jnp.concatenate(
                [scr[ch, pl.ds(SUBLANES - 1, ng, stride=SUBLANES), :] for ch in range(nch)], axis=1)

        ag = group_last(acum, a3)
        bg = group_last(bcum, b3)
        grow_i = lax.broadcasted_iota(jnp.int32, (ng, LRU_WIDTH), 0)
        sh = 1
        while sh < ng:
            a_s = pltpu.roll(ag, sh, axis=0)
            b_s = pltpu.roll(bg, sh, axis=0)
            m = grow_i >= sh
            bg = jnp.where(m, ag * b_s + bg, bg)
            ag = jnp.where(m, ag * a_s, ag)
            sh *= 2
        hend = ag * h0 + bg
        hin = jnp.where(grow_i == 0, h0, pltpu.roll(hend, 1, axis=0))
        hs3 = a3 * hin.reshape(ng, 1, LRU_WIDTH) + b3
        hlast = hend[ng - 1:ng, :]
    hcar[...] = hlast
    nlh_ref[0] = hlast
    y = jnp.dot(hn, wmain_ref[:, s1:s2], preferred_element_type=F32)
    outa_ref[0] = (hs3.reshape(tl, LRU_WIDTH) * _gelu_tanh(y)).astype(BF16)

    xg_buf[SUBLANES:SUBLANES + tl, :] = jnp.dot(hn, wmain_ref[:, s2:s3], preferred_element_type=F32)
    ngc_ref[0] = xg_buf[tl + base:tl + SUBLANES, :]
    qkv = _silu(causal_conv(xg_buf, gcw_ref))
    for h in range(GDN_HEADS):
        qh = qkv[:, h * GDN_DK:(h + 1) * GDN_DK]
        kh = qkv[:, GDN_QK + h * GDN_DK:GDN_QK + (h + 1) * GDN_DK]
        qs = lax.rsqrt(jnp.sum(qh * qh, axis=-1, keepdims=True) + EPS) * (GDN_DK ** -0.5)
        ks = lax.rsqrt(jnp.sum(kh * kh, axis=-1, keepdims=True) + EPS)
        q_ref[0, :, h * GDN_DK:(h + 1) * GDN_DK] = qh * qs
        k_ref[0, :, h * GDN_DK:(h + 1) * GDN_DK] = kh * ks
    v_ref[0] = qkv[:, 2 * GDN_QK:]
    gz_ref[0] = _silu(jnp.dot(hn, wmain_ref[:, s3:], preferred_element_type=F32))

    gc = jnp.dot(hn, wgc_ref[...], preferred_element_type=F32)
    lane = lax.broadcasted_iota(jnp.int32, gc.shape, 1)
    gcol_ref[0] = jnp.where(lane < GDN_HEADS, _sigmoid(gc),
                            -gpc_ref[0:1, :] * _softplus(gc + gpc_ref[1:2, :]))
    gr = lax.dot_general(wgr_ref[...], hn, (((1,), (1,)), ((), ())), preferred_element_type=F32)
    sub = lax.broadcasted_iota(jnp.int32, gr.shape, 0)
    grow_ref[0] = jnp.where(sub < GDN_HEADS, _sigmoid(gr),
                            -gpr_ref[:, 0:1] * _softplus(gr + gpr_ref[:, 1:2]))


def _mixer_in(x, conv_l8, h_l, conv_g8, wts, tl):
    B, L, _ = x.shape
    nl = L // tl
    const2 = lambda b, l: (0, 0)
    const3 = lambda b, l: (0, 0, 0)
    per_b3 = lambda b, l: (b, 0, 0)
    tile3 = lambda b, l: (b, l, 0)
    in_specs = [
        pl.BlockSpec((1, tl, D_MODEL), tile3),
        pl.BlockSpec((1, D_MODEL), const2),
        pl.BlockSpec((D_MODEL, N_MAIN), const2),
        pl.BlockSpec((D_MODEL, N_GATE), const2),
        pl.BlockSpec((N_GATE, D_MODEL), const2),
        pl.BlockSpec((CONV_W, LRU_WIDTH), const2),
        pl.BlockSpec((1, LRU_WIDTH), const2),
        pl.BlockSpec((2, LRU_WIDTH // 2, LRU_WIDTH), const3),
        pl.BlockSpec((1, LRU_WIDTH), const2),
        pl.BlockSpec((1, LRU_WIDTH), const2),
        pl.BlockSpec((1, LRU_WIDTH), const2),
        pl.BlockSpec((CONV_W, GDN_CONV_CH), const2),
        pl.BlockSpec((2, N_GATE), const2),
        pl.BlockSpec((N_GATE, 2), const2),
        pl.BlockSpec((1, SUBLANES, LRU_WIDTH), per_b3),
        pl.BlockSpec((1, 1, LRU_WIDTH), per_b3),
        pl.BlockSpec((1, SUBLANES, GDN_CONV_CH), per_b3),
    ]
    out_shape = (
        jax.ShapeDtypeStruct((B, L, LRU_WIDTH), BF16),
        jax.ShapeDtypeStruct((B, L, GDN_QK), F32),
        jax.ShapeDtypeStruct((B, L, GDN_QK), F32),
        jax.ShapeDtypeStruct((B, L, GDN_V), F32),
        jax.ShapeDtypeStruct((B, L, GDN_V), F32),
        jax.ShapeDtypeStruct((B, L, N_GATE), F32),
        jax.ShapeDtypeStruct((B, N_GATE, L), F32),
        jax.ShapeDtypeStruct((B, CONV_W - 1, LRU_WIDTH), F32),
        jax.ShapeDtypeStruct((B, 1, LRU_WIDTH), F32),
        jax.ShapeDtypeStruct((B, CONV_W - 1, GDN_CONV_CH), F32),
    )
    out_specs = (
        pl.BlockSpec((1, tl, LRU_WIDTH), tile3),
        pl.BlockSpec((1, tl, GDN_QK), tile3),
        pl.BlockSpec((1, tl, GDN_QK), tile3),
        pl.BlockSpec((1, tl, GDN_V), tile3),
        pl.BlockSpec((1, tl, GDN_V), tile3),
        pl.BlockSpec((1, tl, N_GATE), tile3),
        pl.BlockSpec((1, N_GATE, tl), lambda b, l: (b, 0, l)),
        pl.BlockSpec((1, CONV_W - 1, LRU_WIDTH), per_b3),
        pl.BlockSpec((1, 1, LRU_WIDTH), per_b3),
        pl.BlockSpec((1, CONV_W - 1, GDN_CONV_CH), per_b3),
    )
    scratch = [
        pltpu.VMEM((tl + SUBLANES, LRU_WIDTH), F32),
        pltpu.VMEM((tl + SUBLANES, GDN_CONV_CH), F32),
        pltpu.VMEM((1, LRU_WIDTH), F32),
        pltpu.VMEM((LRU_WIDTH // LANES, tl, LANES), F32),
        pltpu.VMEM((LRU_WIDTH // LANES, tl, LANES), F32),
    ]
    return pl.pallas_call(
        functools.partial(_mixer_in_kernel, tl=tl),
        grid=(B, nl), in_specs=in_specs, out_specs=out_specs, out_shape=out_shape,
        scratch_shapes=scratch, compiler_params=_params(("arbitrary", "arbitrary")),
        name="mixer_in",
    )(x, wts["ln1_g"], wts["w_main"], wts["w_gate_col"], wts["w_gate_row"],
      wts["lru_cw"], wts["lru_cb"], wts["lru_wgate"], wts["lru_ba"], wts["lru_bx"], wts["lru_lam"],
      wts["gdn_cw"], wts["gate_par_col"], wts["gate_par_row"],
      conv_l8, h_l, conv_g8)


def _tri_inverse_all(lms, c):
    r = lax.broadcasted_iota(jnp.int32, (c, LANES), 0)
    lane = lax.broadcasted_iota(jnp.int32, (c, LANES), 1)
    col = jnp.bitwise_and(lane, c - 1)
    eye = (r == col).astype(F32)
    odd_block = jnp.bitwise_and(jnp.right_shift(lane, c.bit_length() - 1), 1) == 1

    def same_block(n):
        sh = n.bit_length() - 1
        return jnp.right_shift(r, sh) == jnp.right_shift(col, sh)

    def pieces(a, b):
        ah = a.astype(BF16)
        ah32 = ah.astype(F32)
        mixed = jnp.where(odd_block, a - ah32, ah32).astype(BF16)
        lhs = (mixed if 3 * c <= LANES else jnp.concatenate([mixed, ah], axis=1))[:, :3 * c]
        bh = b.astype(BF16)
        bl = (b - bh.astype(F32)).astype(BF16)
        return lhs, jnp.concatenate([bh, bh, bl], axis=0)

    def dot3_all(xs, ys):
        out = []
        for i in range(0, len(xs), 2):
            (l0, r0), (l1, r1) = pieces(xs[i], ys[i]), pieces(xs[i + 1], ys[i + 1])
            both = jnp.dot(jnp.concatenate([l0, l1], axis=0), jnp.concatenate([r0, r1], axis=1),
                           preferred_element_type=F32)
            out += [both[:c, :LANES], both[c:, LANES:]]
        return out

    blk8 = same_block(SUBLANES)
    l8 = [jnp.where(blk8, lm, 0.0) for lm in lms]
    l2 = dot3_all(l8, l8)
    n1 = [eye - a for a in l8]
    p = [a + b for a, b in zip(n1, dot3_all(n1, l2))]
    l4 = dot3_all(l2, l2)
    p = [a + b for a, b in zip(p, dot3_all(p, l4))]
    n = SUBLANES
    while n < c:
        off = same_block(2 * n) & jnp.logical_not(same_block(n))
        lo = [jnp.where(off, lm, 0.0) for lm in lms]
        t = dot3_all(p, lo)
        p = [a - b for a, b in zip(p, dot3_all(t, p))]
        n *= 2
    return p


def _gdn_kernel(q_ref, k_ref, v_ref, gz_ref, gcol_ref, grow_ref, s0_ref, ng_ref,
                o_ref, sout_ref, s_scr, *, bb, c, cps):
    ct = pl.program_id(1)

    @pl.when(ct == 0)
    def _():
        s_scr[...] = s0_ref[...]

    reps = LANES // c
    r = lax.broadcasted_iota(jnp.int32, (c, LANES), 0)
    col = jnp.bitwise_and(lax.broadcasted_iota(jnp.int32, (c, LANES), 1), c - 1)
    lower = r >= col
    strict = r > col
    rs = lax.broadcasted_iota(jnp.int32, (c, c), 0)
    cs = lax.broadcasted_iota(jnp.int32, (c, c), 1)
    tri_l = (rs >= cs).astype(F32)
    tri_u = jnp.where(r <= col, 1.0, 0.0)

    def hs(h):
        return slice(h * GDN_DK, (h + 1) * GDN_DK)

    def rows(j):
        return slice(j * c, (j + 1) * c)

    chains = [(j, b, h) for j in range(cps) for b in range(bb) for h in range(GDN_HEADS)]
    pairs = [(j, b) for j in range(cps) for b in range(bb)]
    gates_c = {jb: gcol_ref[jb[1], rows(jb[0]), :] for jb in pairs}
    gcum_c = {jb: _dot_exact_lhs(tri_l, gates_c[jb]) for jb in pairs}
    gcum_r = {jb: _dot_exact_rhs(grow_ref[jb[1], jb[0]], tri_u) for jb in pairs}
    beta = [gates_c[(j, b)][:, h:h + 1] for j, b, h in chains]
    gc_c = [gcum_c[(j, b)][:, GDN_HEADS + h:GDN_HEADS + h + 1] for j, b, h in chains]
    gc_r = [gcum_r[(j, b)][GDN_HEADS + h:GDN_HEADS + h + 1, :] for j, b, h in chains]
    decay = [jnp.where(lower, jnp.exp(jnp.where(lower, a - b, 0.0)), 0.0) for a, b in zip(gc_c, gc_r)]
    e_c = [jnp.exp(a) for a in gc_c]
    kb = [k_ref[b, rows(j), hs(h)] * bt for (j, b, h), bt in zip(chains, beta)]
    kk = [_bdot_nt(jnp.concatenate([a, q_ref[b, rows(j), hs(h)]], axis=0),
                   jnp.concatenate([k_ref[b, rows(j), hs(h)]] * reps, axis=0))
          for (j, b, h), a in zip(chains, kb)]
    lmat = [jnp.where(strict, x[:c] * d, 0.0) for x, d in zip(kk, decay)]
    attn = [x[c:, :c] * d[:, :c] for x, d in zip(kk, decay)]
    t_inv = _tri_inverse_all(lmat, c)
    uw = [_bdot(t[:, :c], jnp.concatenate([v_ref[b, rows(j), hs(h)] * bt, a * e], axis=1))
          for (j, b, h), t, bt, a, e in zip(chains, t_inv, beta, kb, e_c)]
    per = bb * GDN_HEADS
    for j in range(cps):
        sel = range(j * per, (j + 1) * per)
        ws = [_bdot(jnp.concatenate([uw[i][:, GDN_DV:],
                                     q_ref[chains[i][1], rows(j), hs(chains[i][2])] * e_c[i]], axis=0),
                    s_scr[chains[i][1], chains[i][2]]) for i in sel]
        v_new = [uw[i][:, :GDN_DV] - x[:c] for i, x in zip(sel, ws)]
        o = [x[c:] + _bdot(attn[i], vn) for i, x, vn in zip(sel, ws, v_new)]
        for i, vn in zip(sel, v_new):
            _, b, h = chains[i]
            g_last = gc_c[i][c - 1:c, :]
            k_dec = k_ref[b, rows(j), hs(h)] * jnp.exp(g_last - gc_c[i])
            s_scr[b, h] = s_scr[b, h] * jnp.exp(g_last) + _bdot_tn(k_dec, vn)
        for i, x in zip(sel, o):
            _, b, h = chains[i]
            on = x * _rms_scale(x) * ng_ref[...]
            o_ref[b, rows(j), hs(h)] = (on * gz_ref[b, rows(j), hs(h)]).astype(BF16)

    sout_ref[...] = s_scr[...]


def _gdn(q, k, v, gz, gcol, grow, s0, ng, bb, c, cps):
    B, L, _ = q.shape
    nc = L // c
    tl = cps * c
    tile = lambda i, t: (i, t, 0)
    per_b4 = lambda i, t: (i, 0, 0, 0)
    in_specs = [
        pl.BlockSpec((bb, tl, GDN_QK), tile),
        pl.BlockSpec((bb, tl, GDN_QK), tile),
        pl.BlockSpec((bb, tl, GDN_V), tile),
        pl.BlockSpec((bb, tl, GDN_V), tile),
        pl.BlockSpec((bb, tl, N_GATE), tile),
        pl.BlockSpec((bb, cps, N_GATE, c), lambda i, t: (i, t, 0, 0)),
        pl.BlockSpec((bb, GDN_HEADS, GDN_DK, GDN_DV), per_b4),
        pl.BlockSpec((1, GDN_DV), lambda i, t: (0, 0)),
    ]
    out_shape = (jax.ShapeDtypeStruct((B, L, GDN_V), BF16),
                 jax.ShapeDtypeStruct((B, GDN_HEADS, GDN_DK, GDN_DV), F32))
    out_specs = (pl.BlockSpec((bb, tl, GDN_V), tile),
                 pl.BlockSpec((bb, GDN_HEADS, GDN_DK, GDN_DV), per_b4))
    return pl.pallas_call(
        functools.partial(_gdn_kernel, bb=bb, c=c, cps=cps),
        grid=(B // bb, nc // cps), in_specs=in_specs, out_specs=out_specs, out_shape=out_shape,
        scratch_shapes=[pltpu.VMEM((bb, GDN_HEADS, GDN_DK, GDN_DV), F32)],
        compiler_params=_params(("arbitrary", "arbitrary")),
        name="gdn",
    )(q, k, v, gz, gcol, grow.reshape(B, N_GATE, nc, c).transpose(0, 2, 1, 3), s0, ng)


def _mixer_out_kernel(oa_ref, og_ref, x_ref, wout_ref, g2_ref, wr_ref, br_ref, cnt0_ref,
                      x1_ref, h2p_ref, ew_ref, ei_ref, cnt_ref, carry, *, tl):
    @pl.when(pl.program_id(0) == 0)
    def _():
        carry[...] = cnt0_ref[...]

    mix = (jnp.dot(oa_ref[...], wout_ref[0:LRU_WIDTH, :], preferred_element_type=F32)
           + jnp.dot(og_ref[...], wout_ref[LRU_WIDTH:, :], preferred_element_type=F32))
    x1 = x_ref[...] + mix
    x1_ref[...] = x1
    h2 = x1 * _rms_scale(x1) * g2_ref[...]
    bits = pltpu.bitcast(h2.astype(BF16).astype(F32), jnp.uint32)
    h2p_ref[...] = (bits[:, :D_MODEL // 2] & jnp.uint32(0xFFFF0000)) | (bits[:, D_MODEL // 2:] >> 16)
    logits = _bdot(h2, wr_ref[...]) + br_ref[...]
    lane = lax.broadcasted_iota(jnp.int32, logits.shape, 1)
    lane_f = lane.astype(F32)
    neg = jnp.float32(-jnp.inf)
    big = jnp.float32(1e9)
    gmask = lane < N_GROUPS
    gl = jnp.where(gmask, logits, neg)
    gmax = jnp.max(gl, axis=1, keepdims=True)
    grp = jnp.min(jnp.where(gl == gmax, lane_f, big), axis=1, keepdims=True)
    p_grp = 1.0 / jnp.sum(jnp.where(gmask, jnp.exp(gl - gmax), 0.0), axis=1, keepdims=True)
    egrp = jnp.right_shift(lane - N_GROUPS, EXPERTS_PER_GROUP.bit_length() - 1).astype(F32)
    sel = (lane >= N_GROUPS) & (lane < N_GROUPS + N_EXPERTS) & (egrp == grp)
    el = jnp.where(sel, logits, neg)
    v1 = jnp.max(el, axis=1, keepdims=True)
    i1 = jnp.min(jnp.where(el == v1, lane_f, big), axis=1, keepdims=True)
    el2 = jnp.where(lane_f == i1, neg, el)
    v2 = jnp.max(el2, axis=1, keepdims=True)
    i2 = jnp.min(jnp.where(el2 == v2, lane_f, big), axis=1, keepdims=True)
    e2 = jnp.exp(v2 - v1)
    den = 1.0 / (1.0 + e2)
    w1 = den * p_grp
    w2 = (e2 * den) * p_grp
    ew_ref[...] = jnp.where(lane == 0, w1, jnp.where(lane == 1, w2, 0.0))

    ex1 = i1 - N_GROUPS
    ex2 = i2 - N_GROUPS
    hit1 = lane_f == ex1
    hit2 = lane_f == ex2
    oh = jnp.concatenate([jnp.where(hit1, 1.0, 0.0), jnp.where(hit2, 1.0, 0.0)], axis=1)
    rr = lax.broadcasted_iota(jnp.int32, (tl, tl), 0)
    cc = lax.broadcasted_iota(jnp.int32, (tl, tl), 1)
    before = jnp.dot(jnp.where(rr > cc, 1.0, 0.0).astype(BF16), oh.astype(BF16),
                     preferred_element_type=F32)
    tot = jnp.sum(oh, axis=0, keepdims=True)
    tot1, tot2 = tot[:, :LANES], tot[:, LANES:]
    c0 = carry[...]
    rank1 = jnp.sum(jnp.where(hit1, c0 + before[:, :LANES], 0.0), axis=1, keepdims=True)
    rank2 = jnp.sum(jnp.where(hit2, (c0 + tot1) + before[:, LANES:], 0.0), axis=1, keepdims=True)
    c1 = c0 + (tot1 + tot2)
    carry[...] = c1
    cnt_ref[...] = c1
    idx = jnp.where(lane == 0, ex1, jnp.where(lane == 1, ex2, jnp.where(
        lane == 2, rank1, jnp.where(lane == 3, rank2, 0.0))))
    ei_ref[...] = jnp.transpose(idx)[0:SUBLANES, :].astype(jnp.int32)


def _mixer_out(oa, og, x, cnt0, wts, tl):
    T = x.shape[0]
    tile = lambda i: (i, 0)
    const = lambda i: (0, 0)
    in_specs = [
        pl.BlockSpec((tl, LRU_WIDTH), tile),
        pl.BlockSpec((tl, GDN_V), tile),
        pl.BlockSpec((tl, D_MODEL), tile),
        pl.BlockSpec((D_MODEL, D_MODEL), const),
        pl.BlockSpec((1, D_MODEL), const),
        pl.BlockSpec((D_MODEL, LANES), const),
        pl.BlockSpec((1, LANES), const),
        pl.BlockSpec((1, LANES), const),
    ]
    out_shape = (jax.ShapeDtypeStruct((T, D_MODEL), F32),
                 jax.ShapeDtypeStruct((T, D_MODEL // 2), jnp.uint32),
                 jax.ShapeDtypeStruct((T, LANES), F32),
                 jax.ShapeDtypeStruct((SUBLANES, T), jnp.int32),
                 jax.ShapeDtypeStruct((1, LANES), F32))
    out_specs = (pl.BlockSpec((tl, D_MODEL), tile), pl.BlockSpec((tl, D_MODEL // 2), tile),
                 pl.BlockSpec((tl, LANES), tile), pl.BlockSpec((SUBLANES, tl), lambda i: (0, i)),
                 pl.BlockSpec((1, LANES), const))
    return pl.pallas_call(
        functools.partial(_mixer_out_kernel, tl=tl), grid=(T // tl,), in_specs=in_specs,
        out_specs=out_specs, out_shape=out_shape, scratch_shapes=[pltpu.VMEM((1, LANES), F32)],
        compiler_params=_params(("arbitrary",)), name="mixer_out",
    )(oa, og, x, wts["w_out"], wts["ln2_g"], wts["w_router"], wts["b_router"], cnt0)


def _row_copy(src, src_row, dst, dst_row, sem):
    return pltpu.make_async_copy(src.at[pl.ds(src_row, 1), :], dst.at[pl.ds(dst_row, 1), :], sem)


def _dispatch_kernel(pe_ref, dest_ref, h2p_ref, *rest, td, rb, n_blocks, chained):
    xs_hbm, zbuf, sem_z, sem_r = rest[1:] if chained else rest
    i = pl.program_id(0)

    def tail_copy(e):
        end = pe_ref[e]
        start = pl.multiple_of(end - rb, rb)
        return pltpu.make_async_copy(zbuf, xs_hbm.at[pl.ds(start, rb), :], sem_z)

    def nonempty(e):
        return pe_ref[e] > (pe_ref[e - 1] if e > 0 else 0)

    def unused_copy(j):
        return pltpu.make_async_copy(zbuf, xs_hbm.at[pl.ds(pl.multiple_of(j * rb, rb), rb), :], sem_z)

    def zero_fill():
        zbuf[...] = jnp.zeros_like(zbuf)
        for e in range(N_EXPERTS):
            pl.when(nonempty(e))(lambda e=e: tail_copy(e).start())
        for e in range(N_EXPERTS):
            pl.when(nonempty(e))(lambda e=e: tail_copy(e).wait())
        first_unused = lax.shift_right_logical(pe_ref[N_EXPERTS - 1], rb.bit_length() - 1)
        lax.fori_loop(first_unused, n_blocks, lambda j, c: (unused_copy(j).start(), c)[1], 0)
        lax.fori_loop(first_unused, n_blocks, lambda j, c: (unused_copy(j).wait(), c)[1], 0)

    if not chained:
        pl.when(i == 0)(zero_fill)

    def copies(r):
        return [_row_copy(h2p_ref, r, xs_hbm, dest_ref[0, 0, k * td + r], sem_r)
                for k in range(TOP_K)]

    for r in range(td):
        for k, cp in enumerate(copies(r)):
            cp.start(priority=k)
    for r in range(td):
        for cp in copies(r):
            cp.wait()


def _dispatch(pad_end, dest_tiles, h2p, n_rows, td, rb, xs_prev=None):
    T = h2p.shape[0]
    chained = xs_prev is not None
    in_specs = [pl.BlockSpec((1, 1, TOP_K * td), lambda i, pe: (i, 0, 0), memory_space=pltpu.SMEM),
                pl.BlockSpec((td, D_MODEL // 2), lambda i, pe: (i, 0))]
    operands = [pad_end, dest_tiles, h2p]
    if chained:
        in_specs.append(pl.BlockSpec(memory_space=pl.ANY))
        operands.append(xs_prev)
    grid_spec = pltpu.PrefetchScalarGridSpec(
        num_scalar_prefetch=1, grid=(T // td,), in_specs=in_specs,
        out_specs=pl.BlockSpec(memory_space=pl.ANY),
        scratch_shapes=[pltpu.VMEM((rb, D_MODEL // 2), jnp.uint32),
                        pltpu.SemaphoreType.DMA(()), pltpu.SemaphoreType.DMA(())],
    )
    return pl.pallas_call(
        functools.partial(_dispatch_kernel, td=td, rb=rb, n_blocks=n_rows // rb, chained=chained),
        grid_spec=grid_spec, out_shape=jax.ShapeDtypeStruct((n_rows, D_MODEL // 2), jnp.uint32),
        input_output_aliases={len(operands) - 1: 0} if chained else {},
        compiler_params=_params(("arbitrary",)), name="dispatch",
    )(*operands)


def _moe_kernel(be_ref, nu_ref, xs_ref, w1_ref, w3_ref, w2_ref, yb_ref):
    del be_ref
    half = D_MODEL // 2

    @pl.when(pl.program_id(0) >= nu_ref[0])
    def _():
        yb_ref[...] = jnp.zeros_like(yb_ref)

    @pl.when(pl.program_id(0) < nu_ref[0])
    def _():
        u = xs_ref[...]
        xa = pltpu.bitcast(u & jnp.uint32(0xFFFF0000), F32).astype(BF16)
        xb = pltpu.bitcast(u << 16, F32).astype(BF16)

        def proj(w_ref):
            return (jnp.dot(xa, w_ref[0, 0:half, :].astype(BF16), preferred_element_type=F32)
                    + jnp.dot(xb, w_ref[0, half:, :].astype(BF16), preferred_element_type=F32))

        hmid = _silu(proj(w1_ref)) * proj(w3_ref)
        yb_ref[...] = _bdot(hmid, w2_ref[0])


def _moe(block_e, n_used, xs, wts, rb):
    n_blocks = block_e.shape[0]

    grid_spec = pltpu.PrefetchScalarGridSpec(
        num_scalar_prefetch=2, grid=(n_blocks,),
        in_specs=[
            pl.BlockSpec((rb, D_MODEL // 2), lambda i, be, nu: (i, 0)),
            pl.BlockSpec((1, D_MODEL, D_EXPERT), lambda i, be, nu: (be[i], 0, 0)),
            pl.BlockSpec((1, D_MODEL, D_EXPERT), lambda i, be, nu: (be[i], 0, 0)),
            pl.BlockSpec((1, D_EXPERT, D_MODEL), lambda i, be, nu: (be[i], 0, 0)),
        ],
        out_specs=pl.BlockSpec((rb, D_MODEL), lambda i, be, nu: (i, 0)),
    )
    return pl.pallas_call(
        _moe_kernel, grid_spec=grid_spec,
        out_shape=jax.ShapeDtypeStruct((n_blocks * rb, D_MODEL), F32),
        compiler_params=_params(("arbitrary",)), name="moe",
    )(block_e, n_used, xs, wts["moe_w1"], wts["moe_w3"], wts["moe_w2"])


def _dispatch_plan(eis, cnt, rb):
    n_tokens = sum(ei.shape[1] for ei in eis)
    n_blocks = -(-(n_tokens * TOP_K + N_EXPERTS * (rb - 1)) // rb)
    counts = cnt[0, :N_EXPERTS].astype(jnp.int32)
    padded = (counts + rb - 1) // rb * rb
    pad_end = jnp.cumsum(padded).astype(jnp.int32)
    pad_start = pad_end - padded
    ei = jnp.concatenate(eis, axis=1)
    e = ei[0:TOP_K, :]
    dest = ei[TOP_K:2 * TOP_K, :]
    for j in range(N_EXPERTS):
        dest = dest + jnp.where(e == j, pad_start[j], 0)
    dests, off = [], 0
    for x in eis:
        dests.append(dest[:, off:off + x.shape[1]])
        off += x.shape[1]
    blk_start = jnp.arange(n_blocks, dtype=jnp.int32) * rb
    block_e = jnp.minimum(jnp.sum(pad_end[None, :] <= blk_start[:, None], axis=1),
                          N_EXPERTS - 1).astype(jnp.int32)
    n_used = (pad_end[N_EXPERTS - 1:] // rb).astype(jnp.int32)
    return pad_end, dests, block_e, n_used, n_blocks * rb


def _dest_tiles(dest, t):
    T = dest.shape[1]
    return dest.reshape(TOP_K, T // t, t).transpose(1, 0, 2).reshape(T // t, 1, TOP_K * t)


def _combine_kernel(dcur_ref, dnext_ref, x1_ref, ew_ref, gf_ref, yb_hbm, y_ref, ybuf, sems,
                    *, tc, n_tiles):
    i = pl.program_id(0)

    def copies(dref, slot, r):
        return [pltpu.make_async_copy(yb_hbm.at[pl.ds(dref[0, 0, k * tc + r], 1), :],
                                      ybuf.at[slot, k, pl.ds(r, 1), :], sems.at[slot])
                for k in range(TOP_K)]

    def gather(dref, slot):
        for r in range(tc):
            for k, cp in enumerate(copies(dref, slot, r)):
                cp.start(priority=k)

    def drain(dref, slot):
        for r in range(tc):
            for cp in copies(dref, slot, r):
                cp.wait()

    pl.when(i == 0)(lambda: gather(dcur_ref, 0))

    def step(slot):
        drain(dcur_ref, slot)
        gather(dnext_ref, 1 - slot)
        w = ew_ref[...]
        moe = w[:, 0:1] * ybuf[slot, 0] + w[:, 1:2] * ybuf[slot, 1]
        x2 = x1_ref[...] + moe
        y_ref[...] = x2 * _rms_scale(x2) * gf_ref[...]
        pl.when(i == n_tiles - 1)(lambda: drain(dnext_ref, 1 - slot))

    pl.when(lax.rem(i, 2) == 0)(lambda: step(0))
    pl.when(lax.rem(i, 2) == 1)(lambda: step(1))


def _combine(dest_tiles, x1, ew, final_g, yb, tc):
    T = x1.shape[0]
    n_tiles = T // tc
    tile = lambda i: (i, 0)
    dspec = lambda f: pl.BlockSpec((1, 1, TOP_K * tc), f, memory_space=pltpu.SMEM)
    return pl.pallas_call(
        functools.partial(_combine_kernel, tc=tc, n_tiles=n_tiles), grid=(n_tiles,),
        in_specs=[dspec(lambda i: (i, 0, 0)),
                  dspec(lambda i: (jnp.minimum(i + 1, n_tiles - 1), 0, 0)),
                  pl.BlockSpec((tc, D_MODEL), tile), pl.BlockSpec((tc, LANES), tile),
                  pl.BlockSpec((1, D_MODEL), lambda i: (0, 0)),
                  pl.BlockSpec(memory_space=pl.ANY)],
        out_specs=pl.BlockSpec((tc, D_MODEL), tile),
        out_shape=jax.ShapeDtypeStruct((T, D_MODEL), F32),
        scratch_shapes=[pltpu.VMEM((2, TOP_K, tc, D_MODEL), F32), pltpu.SemaphoreType.DMA((2,))],
        compiler_params=_params(("arbitrary",)), name="combine",
    )(dest_tiles, dest_tiles, x1, ew, final_g, yb)


def _prep_weights(ln1_g, ln2_g, w_in, lru_conv_w, lru_conv_b, lru_wa, lru_ba, lru_wx, lru_bx,
                  lru_lambda, gdn_conv_w, gdn_a_log, gdn_dt_bias, gdn_norm_g, w_out,
                  router_group_w, router_group_b, router_expert_w, router_expert_b,
                  moe_w1, moe_w3, moe_w2, final_g):
    w_in0 = w_in[0]
    half = LRU_WIDTH // 2
    per_half = half // LRU_BLOCK

    def block_diag(w):
        w4 = w.reshape(2, per_half, LRU_BLOCK, LRU_BLOCK)
        eye = jnp.eye(per_half, dtype=w.dtype)
        return jnp.einsum("snij,nm->snimj", w4, eye).reshape(2, half, half)

    wgate = jnp.concatenate([block_diag(lru_wa[0]), block_diag(lru_wx[0])], axis=2).astype(BF16)
    zeros4 = jnp.zeros((GDN_HEADS,), F32)
    neg_a = jnp.concatenate([zeros4, jnp.exp(gdn_a_log[0].astype(F32))])
    dtb = jnp.concatenate([zeros4, gdn_dt_bias[0].astype(F32)])
    gate_par = jnp.stack([neg_a, dtb])
    w_router = jnp.zeros((D_MODEL, LANES), F32)
    w_router = w_router.at[:, :N_GROUPS].set(router_group_w[0])
    w_router = w_router.at[:, N_GROUPS:N_GROUPS + N_EXPERTS].set(router_expert_w[0])
    b_router = jnp.zeros((1, LANES), F32)
    b_router = b_router.at[0, :N_GROUPS].set(router_group_b[0])
    b_router = b_router.at[0, N_GROUPS:N_GROUPS + N_EXPERTS].set(router_expert_b[0])
    return {
        "ln1_g": ln1_g[0].reshape(1, D_MODEL), "ln2_g": ln2_g[0].reshape(1, D_MODEL),
        "w_main": w_in0.astype(BF16),
        "w_gate_col": w_in0[:, N_MAIN:].astype(BF16),
        "w_gate_row": w_in0[:, N_MAIN:].T.astype(BF16),
        "lru_cw": lru_conv_w[0], "lru_cb": lru_conv_b[0].reshape(1, LRU_WIDTH),
        "lru_wgate": wgate,
        "lru_ba": lru_ba[0].reshape(1, LRU_WIDTH), "lru_bx": lru_bx[0].reshape(1, LRU_WIDTH),
        "lru_lam": lru_lambda[0].reshape(1, LRU_WIDTH),
        "gdn_cw": gdn_conv_w[0], "gate_par_col": gate_par, "gate_par_row": gate_par.T,
        "gdn_ng": gdn_norm_g[0].reshape(1, GDN_DV),
        "w_out": w_out[0].astype(BF16),
        "w_router": w_router, "b_router": b_router,
        "moe_w1": moe_w1[0], "moe_w3": moe_w3[0], "moe_w2": moe_w2[0],
        "final_g": final_g.reshape(1, D_MODEL),
    }


def _pad_conv_state(s):
    return jnp.pad(s, ((0, 0), (SUBLANES - (CONV_W - 1), 0), (0, 0)))


def _mixer(x, conv_l, h_l, conv_g, s_g, cnt0, wts, tl_in, bb, cps, tl_tok):
    B, L, _ = x.shape
    T = B * L
    c = min(CHUNK, L)
    oa, q, k, v, gz, gcol, grow, nlc, nlh, ngc = _mixer_in(
        x, _pad_conv_state(conv_l), h_l.reshape(B, 1, LRU_WIDTH), _pad_conv_state(conv_g), wts, tl_in)
    og, ns = _gdn(q, k, v, gz, gcol, grow, s_g, wts["gdn_ng"], bb, c, cps)
    routed = _mixer_out(oa.reshape(T, LRU_WIDTH), og.reshape(T, GDN_V), x.reshape(T, D_MODEL),
                        cnt0, wts, tl_tok)
    return routed, (nlc[None], nlh.reshape(1, B, LRU_WIDTH), ngc[None], ns[None])


def kernel(x_prompt, x_sample, state_lru_conv, state_lru_h, state_gdn_conv, state_gdn_S, ln1_g, ln2_g, w_in, lru_conv_w, lru_conv_b, lru_wa, lru_ba, lru_wx, lru_bx, lru_lambda, gdn_conv_w, gdn_a_log, gdn_dt_bias, gdn_norm_g, w_out, router_group_w, router_group_b, router_expert_w, router_expert_b, moe_w1, moe_w3, moe_w2, final_g):
    wts = _prep_weights(ln1_g, ln2_g, w_in, lru_conv_w, lru_conv_b, lru_wa, lru_ba, lru_wx, lru_bx,
                        lru_lambda, gdn_conv_w, gdn_a_log, gdn_dt_bias, gdn_norm_g, w_out,
                        router_group_w, router_group_b, router_expert_w, router_expert_b,
                        moe_w1, moe_w3, moe_w2, final_g)
    B, L, _ = x_prompt.shape
    Bs, Ls, _ = x_sample.shape
    dt = x_prompt.dtype
    tile_p, tile_s = 512, Bs * Ls
    row_block = 512
    routed_p, states_p = _mixer(
        x_prompt,
        jnp.zeros((B, CONV_W - 1, LRU_WIDTH), dt), jnp.zeros((B, LRU_WIDTH), dt),
        jnp.zeros((B, CONV_W - 1, GDN_CONV_CH), dt), jnp.zeros((B, GDN_HEADS, GDN_DK, GDN_DV), dt),
        jnp.zeros((1, LANES), F32), wts, tl_in=512, bb=4, cps=2, tl_tok=tile_p)
    x1_p, h2p_p, ew_p, ei_p, cnt_p = routed_p
    routed_s, states_s = _mixer(
        x_sample, state_lru_conv[0], state_lru_h[0], state_gdn_conv[0], state_gdn_S[0],
        cnt_p, wts, tl_in=Ls, bb=2, cps=1, tl_tok=tile_s)
    x1_s, h2p_s, ew_s, ei_s, cnt = routed_s

    pad_end, (dest_p, dest_s), block_e, n_used, n_rows = _dispatch_plan([ei_p, ei_s], cnt, row_block)
    xs = _dispatch(pad_end, _dest_tiles(dest_p, tile_p), h2p_p, n_rows, tile_p, row_block)
    xs = _dispatch(pad_end, _dest_tiles(dest_s, tile_s), h2p_s, n_rows, tile_s, row_block, xs_prev=xs)
    yb = _moe(block_e, n_used, xs, wts, row_block)
    y_p = _combine(_dest_tiles(dest_p, tile_p), x1_p, ew_p, wts["final_g"], yb, tile_p)
    y_s = _combine(_dest_tiles(dest_s, tile_s), x1_s, ew_s, wts["final_g"], yb, tile_s)
    return (y_p.reshape(B, L, D_MODEL), y_s.reshape(Bs, Ls, D_MODEL)) + states_p + states_s
```

```python
import functools

import jax
import jax.numpy as jnp
from jax import lax
from jax.experimental import pallas as pl
from jax.experimental.pallas import tpu as pltpu

F32 = jnp.float32
BF16 = jnp.bfloat16

D_MODEL = 1024
CONV_W = 4
LRU_WIDTH = 512
LRU_BLOCKS = 8
LRU_BLOCK = LRU_WIDTH // LRU_BLOCKS
LRU_C = 8.0
GDN_HEADS = 4
GDN_DK = 128
GDN_DV = 128
GDN_QK = GDN_HEADS * GDN_DK
GDN_V = GDN_HEADS * GDN_DV
GDN_CONV_CH = 2 * GDN_QK + GDN_V
N_GATE = 2 * GDN_HEADS
N_MAIN = 2 * LRU_WIDTH + GDN_CONV_CH + GDN_V
CHUNK = 64
N_GROUPS = 4
EXPERTS_PER_GROUP = 8
N_EXPERTS = N_GROUPS * EXPERTS_PER_GROUP
TOP_K = 2
D_EXPERT = 256
EPS = 1e-6

SUBLANES = 8
LANES = 128
VMEM_LIMIT_BYTES = 48 * 1024 * 1024


def _bdot(a, b):
    return jnp.dot(a.astype(BF16), b.astype(BF16), preferred_element_type=F32)


def _bdot_nt(a, b):
    return lax.dot_general(a.astype(BF16), b.astype(BF16), (((1,), (1,)), ((), ())),
                           preferred_element_type=F32)


def _bdot_tn(a, b):
    return lax.dot_general(a.astype(BF16), b.astype(BF16), (((0,), (0,)), ((), ())),
                           preferred_element_type=F32)


def _dot_exact_rhs(a, b01):
    b = b01.astype(BF16)
    a1 = a.astype(BF16)
    r1 = a - a1.astype(F32)
    a2 = r1.astype(BF16)
    a3 = (r1 - a2.astype(F32)).astype(BF16)
    d = functools.partial(jnp.dot, preferred_element_type=F32)
    return d(a1, b) + (d(a2, b) + d(a3, b))


def _dot_exact_lhs(a01, b):
    a = a01.astype(BF16)
    b1 = b.astype(BF16)
    r1 = b - b1.astype(F32)
    b2 = r1.astype(BF16)
    b3 = (r1 - b2.astype(F32)).astype(BF16)
    d = functools.partial(jnp.dot, preferred_element_type=F32)
    return d(a, b1) + (d(a, b2) + d(a, b3))


NEG_LOG2_E = -1.4426950408889634


def _sigmoid(x):
    return 1.0 / (1.0 + jnp.exp2(x * NEG_LOG2_E))


def _softplus(x):
    return jnp.maximum(x, 0.0) + jnp.log(1.0 + jnp.exp(-jnp.abs(x)))


def _silu(x):
    return x * _sigmoid(x)


def _gelu_tanh(x):
    c = 0.7978845608028654
    return 0.5 * x * (1.0 + jnp.tanh(c * (x + 0.044715 * (x * x * x))))


def _rms_scale(x):
    return lax.rsqrt(jnp.mean(x * x, axis=-1, keepdims=True) + EPS)


def _params(sem):
    return pltpu.CompilerParams(dimension_semantics=sem, vmem_limit_bytes=VMEM_LIMIT_BYTES)


def _mixer_in_kernel(x_ref, g1_ref, wmain_ref, wgc_ref, wgr_ref,
                     lcw_ref, lcb_ref, wgate_ref, ba_ref, bx_ref, lam_ref,
                     gcw_ref, gpc_ref, gpr_ref,
                     lconv0_ref, lh0_ref, gconv0_ref,
                     outa_ref, q_ref, k_ref, v_ref, gz_ref, gcol_ref, grow_ref,
                     nlc_ref, nlh_ref, ngc_ref,
                     xl_buf, xg_buf, hcar, acum, bcum, *, tl):
    lt = pl.program_id(1)
    s1, s2, s3 = LRU_WIDTH, 2 * LRU_WIDTH, 2 * LRU_WIDTH + GDN_CONV_CH

    @pl.when(lt == 0)
    def _():
        xl_buf[0:SUBLANES, :] = lconv0_ref[0]
        xg_buf[0:SUBLANES, :] = gconv0_ref[0]
        hcar[...] = lh0_ref[0]

    @pl.when(lt > 0)
    def _():
        xl_buf[0:SUBLANES, :] = xl_buf[tl:tl + SUBLANES, :]
        xg_buf[0:SUBLANES, :] = xg_buf[tl:tl + SUBLANES, :]

    x = x_ref[0]
    hn = (x * _rms_scale(x) * g1_ref[...]).astype(BF16)

    xl_buf[SUBLANES:SUBLANES + tl, :] = jnp.dot(hn, wmain_ref[:, 0:s1], preferred_element_type=F32)
    base = SUBLANES - (CONV_W - 1)

    def causal_conv(buf, w_ref):
        xall = buf[...]
        acc = None
        for j in range(CONV_W):
            back = CONV_W - 1 - j
            shifted = pltpu.roll(xall, back, axis=0) if back else xall
            term = shifted[SUBLANES:, :] * w_ref[j:j + 1, :]
            acc = term if acc is None else acc + term
        return acc

    xc = causal_conv(xl_buf, lcw_ref) + lcb_ref[...]
    nlc_ref[0] = xl_buf[tl + base:tl + SUBLANES, :]

    half = LRU_WIDTH // 2
    pre = [_bdot(xc[:, s * half:(s + 1) * half], wgate_ref[s]) for s in range(2)]
    r_pre = jnp.concatenate([pre[0][:, :half], pre[1][:, :half]], axis=1)
    i_pre = jnp.concatenate([pre[0][:, half:], pre[1][:, half:]], axis=1)
    r = _sigmoid(r_pre + ba_ref[...])
    ig = _sigmoid(i_pre + bx_ref[...])
    a = jnp.exp2(r * (_softplus(-lam_ref[...]) * (LRU_C * NEG_LOG2_E)))
    b = jnp.sqrt(1.0 - a * a) * (ig * xc)

    ng = tl // SUBLANES
    a3 = a.reshape(ng, SUBLANES, LRU_WIDTH)
    b3 = b.reshape(ng, SUBLANES, LRU_WIDTH)
    row = lax.broadcasted_iota(jnp.int32, (ng, SUBLANES, LRU_WIDTH), 1)
    sh = 1
    while sh < SUBLANES:
        a_s = pltpu.roll(a3, sh, axis=1)
        b_s = pltpu.roll(b3, sh, axis=1)
        m = row >= sh
        b3 = jnp.where(m, a3 * b_s + b3, b3)
        a3 = jnp.where(m, a3 * a_s, a3)
        sh *= 2
    h0 = hcar[...]
    if ng % SUBLANES:
        parts = []
        hlast = h0
        for g in range(ng):
            hg = a3[g] * hlast + b3[g]
            parts.append(hg)
            hlast = hg[SUBLANES - 1:SUBLANES, :]
        hs3 = jnp.stack(parts)
    else:
        def group_last(scr, val3):
            val = val3.reshape(tl, LRU_WIDTH)
            nch = LRU_WIDTH // LANES
            for ch in range(nch):
                scr[ch] = val[:, ch * LANES:(ch + 1) * LANES]
            return jnp.concatenate(
                [scr[ch, pl.ds(SUBLANES - 1, ng, stride=SUBLANES), :] for ch in range(nch)], axis=1)

        ag = group_last(acum, a3)
        bg = group_last(bcum, b3)
        grow_i = lax.broadcasted_iota(jnp.int32, (ng, LRU_WIDTH), 0)
        sh = 1
        while sh < ng:
            a_s = pltpu.roll(ag, sh, axis=0)
            b_s = pltpu.roll(bg, sh, axis=0)
            m = grow_i >= sh
            bg = jnp.where(m, ag * b_s + bg, bg)
            ag = jnp.where(m, ag * a_s, ag)
            sh *= 2
        hend = ag * h0 + bg
        hin = jnp.where(grow_i == 0, h0, pltpu.roll(hend, 1, axis=0))
        hs3 = a3 * hin.reshape(ng, 1, LRU_WIDTH) + b3
        hlast = hend[ng - 1:ng, :]
    hcar[...] = hlast
    nlh_ref[0] = hlast
    y = jnp.dot(hn, wmain_ref[:, s1:s2], preferred_element_type=F32)
    outa_ref[0] = (hs3.reshape(tl, LRU_WIDTH) * _gelu_tanh(y)).astype(BF16)

    xg_buf[SUBLANES:SUBLANES + tl, :] = jnp.dot(hn, wmain_ref[:, s2:s3], preferred_element_type=F32)
    ngc_ref[0] = xg_buf[tl + base:tl + SUBLANES, :]
    qkv = _silu(causal_conv(xg_buf, gcw_ref))
    for h in range(GDN_HEADS):
        qh = qkv[:, h * GDN_DK:(h + 1) * GDN_DK]
        kh = qkv[:, GDN_QK + h * GDN_DK:GDN_QK + (h + 1) * GDN_DK]
        qs = lax.rsqrt(jnp.sum(qh * qh, axis=-1, keepdims=True) + EPS) * (GDN_DK ** -0.5)
        ks = lax.rsqrt(jnp.sum(kh * kh, axis=-1, keepdims=True) + EPS)
        q_ref[0, :, h * GDN_DK:(h + 1) * GDN_DK] = qh * qs
        k_ref[0, :, h * GDN_DK:(h + 1) * GDN_DK] = kh * ks
    v_ref[0] = qkv[:, 2 * GDN_QK:]
    gz_ref[0] = _silu(jnp.dot(hn, wmain_ref[:, s3:], preferred_element_type=F32))

    gc = jnp.dot(hn, wgc_ref[...], preferred_element_type=F32)
    lane = lax.broadcasted_iota(jnp.int32, gc.shape, 1)
    gcol_ref[0] = jnp.where(lane < GDN_HEADS, _sigmoid(gc),
                            -gpc_ref[0:1, :] * _softplus(gc + gpc_ref[1:2, :]))
    gr = lax.dot_general(wgr_ref[...], hn, (((1,), (1,)), ((), ())), preferred_element_type=F32)
    sub = lax.broadcasted_iota(jnp.int32, gr.shape, 0)
    grow_ref[0] = jnp.where(sub < GDN_HEADS, _sigmoid(gr),
                            -gpr_ref[:, 0:1] * _softplus(gr + gpr_ref[:, 1:2]))


def _mixer_in(x, conv_l8, h_l, conv_g8, wts, tl):
    B, L, _ = x.shape
    nl = L // tl
    const2 = lambda b, l: (0, 0)
    const3 = lambda b, l: (0, 0, 0)
    per_b3 = lambda b, l: (b, 0, 0)
    tile3 = lambda b, l: (b, l, 0)
    in_specs = [
        pl.BlockSpec((1, tl, D_MODEL), tile3),
        pl.BlockSpec((1, D_MODEL), const2),
        pl.BlockSpec((D_MODEL, N_MAIN), const2),
        pl.BlockSpec((D_MODEL, N_GATE), const2),
        pl.BlockSpec((N_GATE, D_MODEL), const2),
        pl.BlockSpec((CONV_W, LRU_WIDTH), const2),
        pl.BlockSpec((1, LRU_WIDTH), const2),
        pl.BlockSpec((2, LRU_WIDTH // 2, LRU_WIDTH), const3),
        pl.BlockSpec((1, LRU_WIDTH), const2),
        pl.BlockSpec((1, LRU_WIDTH), const2),
        pl.BlockSpec((1, LRU_WIDTH), const2),
        pl.BlockSpec((CONV_W, GDN_CONV_CH), const2),
        pl.BlockSpec((2, N_GATE), const2),
        pl.BlockSpec((N_GATE, 2), const2),
        pl.BlockSpec((1, SUBLANES, LRU_WIDTH), per_b3),
        pl.BlockSpec((1, 1, LRU_WIDTH), per_b3),
        pl.BlockSpec((1, SUBLANES, GDN_CONV_CH), per_b3),
    ]
    out_shape = (
        jax.ShapeDtypeStruct((B, L, LRU_WIDTH), BF16),
        jax.ShapeDtypeStruct((B, L, GDN_QK), F32),
        jax.ShapeDtypeStruct((B, L, GDN_QK), F32),
        jax.ShapeDtypeStruct((B, L, GDN_V), F32),
        jax.ShapeDtypeStruct((B, L, GDN_V), F32),
        jax.ShapeDtypeStruct((B, L, N_GATE), F32),
        jax.ShapeDtypeStruct((B, N_GATE, L), F32),
        jax.ShapeDtypeStruct((B, CONV_W - 1, LRU_WIDTH), F32),
        jax.ShapeDtypeStruct((B, 1, LRU_WIDTH), F32),
        jax.ShapeDtypeStruct((B, CONV_W - 1, GDN_CONV_CH), F32),
    )
    out_specs = (
        pl.BlockSpec((1, tl, LRU_WIDTH), tile3),
        pl.BlockSpec((1, tl, GDN_QK), tile3),
        pl.BlockSpec((1, tl, GDN_QK), tile3),
        pl.BlockSpec((1, tl, GDN_V), tile3),
        pl.BlockSpec((1, tl, GDN_V), tile3),
        pl.BlockSpec((1, tl, N_GATE), tile3),
        pl.BlockSpec((1, N_GATE, tl), lambda b, l: (b, 0, l)),
        pl.BlockSpec((1, CONV_W - 1, LRU_WIDTH), per_b3),
        pl.BlockSpec((1, 1, LRU_WIDTH), per_b3),
        pl.BlockSpec((1, CONV_W - 1, GDN_CONV_CH), per_b3),
    )
    scratch = [
        pltpu.VMEM((tl + SUBLANES, LRU_WIDTH), F32),
        pltpu.VMEM((tl + SUBLANES, GDN_CONV_CH), F32),
        pltpu.VMEM((1, LRU_WIDTH), F32),
        pltpu.VMEM((LRU_WIDTH // LANES, tl, LANES), F32),
        pltpu.VMEM((LRU_WIDTH // LANES, tl, LANES), F32),
    ]
    return pl.pallas_call(
        functools.partial(_mixer_in_kernel, tl=tl),
        grid=(B, nl), in_specs=in_specs, out_specs=out_specs, out_shape=out_shape,
        scratch_shapes=scratch, compiler_params=_params(("arbitrary", "arbitrary")),
        name="mixer_in",
    )(x, wts["ln1_g"], wts["w_main"], wts["w_gate_col"], wts["w_gate_row"],
      wts["lru_cw"], wts["lru_cb"], wts["lru_wgate"], wts["lru_ba"], wts["lru_bx"], wts["lru_lam"],
      wts["gdn_cw"], wts["gate_par_col"], wts["gate_par_row"],
      conv_l8, h_l, conv_g8)


def _tri_inverse_all(lms, c):
    r = lax.broadcasted_iota(jnp.int32, (c, LANES), 0)
    lane = lax.broadcasted_iota(jnp.int32, (c, LANES), 1)
    col = jnp.bitwise_and(lane, c - 1)
    eye = (r == col).astype(F32)
    odd_block = jnp.bitwise_and(jnp.right_shift(lane, c.bit_length() - 1), 1) == 1

    def same_block(n):
        sh = n.bit_length() - 1
        return jnp.right_shift(r, sh) == jnp.right_shift(col, sh)

    def pieces(a, b):
        ah = a.astype(BF16)
        ah32 = ah.astype(F32)
        mixed = jnp.where(odd_block, a - ah32, ah32).astype(BF16)
        lhs = (mixed if 3 * c <= LANES else jnp.concatenate([mixed, ah], axis=1))[:, :3 * c]
        bh = b.astype(BF16)
        bl = (b - bh.astype(F32)).astype(BF16)
        return lhs, jnp.concatenate([bh, bh, bl], axis=0)

    def dot3_all(xs, ys):
        out = []
        for i in range(0, len(xs), 2):
            (l0, r0), (l1, r1) = pieces(xs[i], ys[i]), pieces(xs[i + 1], ys[i + 1])
            both = jnp.dot(jnp.concatenate([l0, l1], axis=0), jnp.concatenate([r0, r1], axis=1),
                           preferred_element_type=F32)
            out += [both[:c, :LANES], both[c:, LANES:]]
        return out

    blk8 = same_block(SUBLANES)
    l8 = [jnp.where(blk8, lm, 0.0) for lm in lms]
    l2 = dot3_all(l8, l8)
    n1 = [eye - a for a in l8]
    p = [a + b for a, b in zip(n1, dot3_all(n1, l2))]
    l4 = dot3_all(l2, l2)
    p = [a + b for a, b in zip(p, dot3_all(p, l4))]
    n = SUBLANES
    while n < c:
        off = same_block(2 * n) & jnp.logical_not(same_block(n))
        lo = [jnp.where(off, lm, 0.0) for lm in lms]
        t = dot3_all(p, lo)
        p = [a - b for a, b in zip(p, dot3_all(t, p))]
        n *= 2
    return p


def _gdn_kernel(q_ref, k_ref, v_ref, gz_ref, gcol_ref, grow_ref, s0_ref, ng_ref,
                o_ref, sout_ref, s_scr, *, bb, c, cps):
    ct = pl.program_id(1)

    @pl.when(ct == 0)
    def _():
        s_scr[...] = s0_ref[...]

    reps = LANES // c
    r = lax.broadcasted_iota(jnp.int32, (c, LANES), 0)
    col = jnp.bitwise_and(lax.broadcasted_iota(jnp.int32, (c, LANES), 1), c - 1)
    lower = r >= col
    strict = r > col
    rs = lax.broadcasted_iota(jnp.int32, (c, c), 0)
    cs = lax.broadcasted_iota(jnp.int32, (c, c), 1)
    tri_l = (rs >= cs).astype(F32)
    tri_u = jnp.where(r <= col, 1.0, 0.0)

    def hs(h):
        return slice(h * GDN_DK, (h + 1) * GDN_DK)

    def rows(j):
        return slice(j * c, (j + 1) * c)

    chains = [(j, b, h) for j in range(cps) for b in range(bb) for h in range(GDN_HEADS)]
    pairs = [(j, b) for j in range(cps) for b in range(bb)]
    gates_c = {jb: gcol_ref[jb[1], rows(jb[0]), :] for jb in pairs}
    gcum_c = {jb: _dot_exact_lhs(tri_l, gates_c[jb]) for jb in pairs}
    gcum_r = {jb: _dot_exact_rhs(grow_ref[jb[1], jb[0]], tri_u) for jb in pairs}
    beta = [gates_c[(j, b)][:, h:h + 1] for j, b, h in chains]
    gc_c = [gcum_c[(j, b)][:, GDN_HEADS + h:GDN_HEADS + h + 1] for j, b, h in chains]
    gc_r = [gcum_r[(j, b)][GDN_HEADS + h:GDN_HEADS + h + 1, :] for j, b, h in chains]
    decay = [jnp.where(lower, jnp.exp(jnp.where(lower, a - b, 0.0)), 0.0) for a, b in zip(gc_c, gc_r)]
    e_c = [jnp.exp(a) for a in gc_c]
    kb = [k_ref[b, rows(j), hs(h)] * bt for (j, b, h), bt in zip(chains, beta)]
    kk = [_bdot_nt(jnp.concatenate([a, q_ref[b, rows(j), hs(h)]], axis=0),
                   jnp.concatenate([k_ref[b, rows(j), hs(h)]] * reps, axis=0))
          for (j, b, h), a in zip(chains, kb)]
    lmat = [jnp.where(strict, x[:c] * d, 0.0) for x, d in zip(kk, decay)]
    attn = [x[c:, :c] * d[:, :c] for x, d in zip(kk, decay)]
    t_inv = _tri_inverse_all(lmat, c)
    uw = [_bdot(t[:, :c], jnp.concatenate([v_ref[b, rows(j), hs(h)] * bt, a * e], axis=1))
          for (j, b, h), t, bt, a, e in zip(chains, t_inv, beta, kb, e_c)]
    per = bb * GDN_HEADS
    for j in range(cps):
        sel = range(j * per, (j + 1) * per)
        ws = [_bdot(jnp.concatenate([uw[i][:, GDN_DV:],
                                     q_ref[chains[i][1], rows(j), hs(chains[i][2])] * e_c[i]], axis=0),
                    s_scr[chains[i][1], chains[i][2]]) for i in sel]
        v_new = [uw[i][:, :GDN_DV] - x[:c] for i, x in zip(sel, ws)]
        o = [x[c:] + _bdot(attn[i], vn) for i, x, vn in zip(sel, ws, v_new)]
        for i, vn in zip(sel, v_new):
            _, b, h = chains[i]
            g_last = gc_c[i][c - 1:c, :]
            k_dec = k_ref[b, rows(j), hs(h)] * jnp.exp(g_last - gc_c[i])
            s_scr[b, h] = s_scr[b, h] * jnp.exp(g_last) + _bdot_tn(k_dec, vn)
        for i, x in zip(sel, o):
            _, b, h = chains[i]
            on = x * _rms_scale(x) * ng_ref[...]
            o_ref[b, rows(j), hs(h)] = (on * gz_ref[b, rows(j), hs(h)]).astype(BF16)

    sout_ref[...] = s_scr[...]


def _gdn(q, k, v, gz, gcol, grow, s0, ng, bb, c, cps):
    B, L, _ = q.shape
    nc = L // c
    tl = cps * c
    tile = lambda i, t: (i, t, 0)
    per_b4 = lambda i, t: (i, 0, 0, 0)
    in_specs = [
        pl.BlockSpec((bb, tl, GDN_QK), tile),
        pl.BlockSpec((bb, tl, GDN_QK), tile),
        pl.BlockSpec((bb, tl, GDN_V), tile),
        pl.BlockSpec((bb, tl, GDN_V), tile),
        pl.BlockSpec((bb, tl, N_GATE), tile),
        pl.BlockSpec((bb, cps, N_GATE, c), lambda i, t: (i, t, 0, 0)),
        pl.BlockSpec((bb, GDN_HEADS, GDN_DK, GDN_DV), per_b4),
        pl.BlockSpec((1, GDN_DV), lambda i, t: (0, 0)),
    ]
    out_shape = (jax.ShapeDtypeStruct((B, L, GDN_V), BF16),
                 jax.ShapeDtypeStruct((B, GDN_HEADS, GDN_DK, GDN_DV), F32))
    out_specs = (pl.BlockSpec((bb, tl, GDN_V), tile),
                 pl.BlockSpec((bb, GDN_HEADS, GDN_DK, GDN_DV), per_b4))
    return pl.pallas_call(
        functools.partial(_gdn_kernel, bb=bb, c=c, cps=cps),
        grid=(B // bb, nc // cps), in_specs=in_specs, out_specs=out_specs, out_shape=out_shape,
        scratch_shapes=[pltpu.VMEM((bb, GDN_HEADS, GDN_DK, GDN_DV), F32)],
        compiler_params=_params(("arbitrary", "arbitrary")),
        name="gdn",
    )(q, k, v, gz, gcol, grow.reshape(B, N_GATE, nc, c).transpose(0, 2, 1, 3), s0, ng)


def _mixer_out_kernel(oa_ref, og_ref, x_ref, wout_ref, g2_ref, wr_ref, br_ref, cnt0_ref,
                      x1_ref, h2p_ref, ew_ref, ei_ref, cnt_ref, carry, *, tl):
    @pl.when(pl.program_id(0) == 0)
    def _():
        carry[...] = cnt0_ref[...]

    mix = (jnp.dot(oa_ref[...], wout_ref[0:LRU_WIDTH, :], preferred_element_type=F32)
           + jnp.dot(og_ref[...], wout_ref[LRU_WIDTH:, :], preferred_element_type=F32))
    x1 = x_ref[...] + mix
    x1_ref[...] = x1
    h2 = x1 * _rms_scale(x1) * g2_ref[...]
    bits = pltpu.bitcast(h2.astype(BF16).astype(F32), jnp.uint32)
    h2p_ref[...] = (bits[:, :D_MODEL // 2] & jnp.uint32(0xFFFF0000)) | (bits[:, D_MODEL // 2:] >> 16)
    logits = _bdot(h2, wr_ref[...]) + br_ref[...]
    lane = lax.broadcasted_iota(jnp.int32, logits.shape, 1)
    lane_f = lane.astype(F32)
    neg = jnp.float32(-jnp.inf)
    big = jnp.float32(1e9)
    gmask = lane < N_GROUPS
    gl = jnp.where(gmask, logits, neg)
    gmax = jnp.max(gl, axis=1, keepdims=True)
    grp = jnp.min(jnp.where(gl == gmax, lane_f, big), axis=1, keepdims=True)
    p_grp = 1.0 / jnp.sum(jnp.where(gmask, jnp.exp(gl - gmax), 0.0), axis=1, keepdims=True)
    egrp = jnp.right_shift(lane - N_GROUPS, EXPERTS_PER_GROUP.bit_length() - 1).astype(F32)
    sel = (lane >= N_GROUPS) & (lane < N_GROUPS + N_EXPERTS) & (egrp == grp)
    el = jnp.where(sel, logits, neg)
    v1 = jnp.max(el, axis=1, keepdims=True)
    i1 = jnp.min(jnp.where(el == v1, lane_f, big), axis=1, keepdims=True)
    el2 = jnp.where(lane_f == i1, neg, el)
    v2 = jnp.max(el2, axis=1, keepdims=True)
    i2 = jnp.min(jnp.where(el2 == v2, lane_f, big), axis=1, keepdims=True)
    e2 = jnp.exp(v2 - v1)
    den = 1.0 / (1.0 + e2)
    w1 = den * p_grp
    w2 = (e2 * den) * p_grp
    ew_ref[...] = jnp.where(lane == 0, w1, jnp.where(lane == 1, w2, 0.0))

    ex1 = i1 - N_GROUPS
    ex2 = i2 - N_GROUPS
    hit1 = lane_f == ex1
    hit2 = lane_f == ex2
    oh = jnp.concatenate([jnp.where(hit1, 1.0, 0.0), jnp.where(hit2, 1.0, 0.0)], axis=1)
    rr = lax.broadcasted_iota(jnp.int32, (tl, tl), 0)
    cc = lax.broadcasted_iota(jnp.int32, (tl, tl), 1)
    before = jnp.dot(jnp.where(rr > cc, 1.0, 0.0).astype(BF16), oh.astype(BF16),
                     preferred_element_type=F32)
    tot = jnp.sum(oh, axis=0, keepdims=True)
    tot1, tot2 = tot[:, :LANES], tot[:, LANES:]
    c0 = carry[...]
    rank1 = jnp.sum(jnp.where(hit1, c0 + before[:, :LANES], 0.0), axis=1, keepdims=True)
    rank2 = jnp.sum(jnp.where(hit2, (c0 + tot1) + before[:, LANES:], 0.0), axis=1, keepdims=True)
    c1 = c0 + (tot1 + tot2)
    carry[...] = c1
    cnt_ref[...] = c1
    idx = jnp.where(lane == 0, ex1, jnp.where(lane == 1, ex2, jnp.where(
        lane == 2, rank1, jnp.where(lane == 3, rank2, 0.0))))
    ei_ref[...] = jnp.transpose(idx)[0:SUBLANES, :].astype(jnp.int32)


def _mixer_out(oa, og, x, cnt0, wts, tl):
    T = x.shape[0]
    tile = lambda i: (i, 0)
    const = lambda i: (0, 0)
    in_specs = [
        pl.BlockSpec((tl, LRU_WIDTH), tile),
        pl.BlockSpec((tl, GDN_V), tile),
        pl.BlockSpec((tl, D_MODEL), tile),
        pl.BlockSpec((D_MODEL, D_MODEL), const),
        pl.BlockSpec((1, D_MODEL), const),
        pl.BlockSpec((D_MODEL, LANES), const),
        pl.BlockSpec((1, LANES), const),
        pl.BlockSpec((1, LANES), const),
    ]
    out_shape = (jax.ShapeDtypeStruct((T, D_MODEL), F32),
                 jax.ShapeDtypeStruct((T, D_MODEL // 2), jnp.uint32),
                 jax.ShapeDtypeStruct((T, LANES), F32),
                 jax.ShapeDtypeStruct((SUBLANES, T), jnp.int32),
                 jax.ShapeDtypeStruct((1, LANES), F32))
    out_specs = (pl.BlockSpec((tl, D_MODEL), tile), pl.BlockSpec((tl, D_MODEL // 2), tile),
                 pl.BlockSpec((tl, LANES), tile), pl.BlockSpec((SUBLANES, tl), lambda i: (0, i)),
                 pl.BlockSpec((1, LANES), const))
    return pl.pallas_call(
        functools.partial(_mixer_out_kernel, tl=tl), grid=(T // tl,), in_specs=in_specs,
        out_specs=out_specs, out_shape=out_shape, scratch_shapes=[pltpu.VMEM((1, LANES), F32)],
        compiler_params=_params(("arbitrary",)), name="mixer_out",
    )(oa, og, x, wts["w_out"], wts["ln2_g"], wts["w_router"], wts["b_router"], cnt0)


def _row_copy(src, src_row, dst, dst_row, sem):
    return pltpu.make_async_copy(src.at[pl.ds(src_row, 1), :], dst.at[pl.ds(dst_row, 1), :], sem)


def _dispatch_kernel(pe_ref, dest_ref, h2p_ref, *rest, td, rb, n_blocks, chained):
    xs_hbm, zbuf, sem_z, sem_r = rest[1:] if chained else rest
    i = pl.program_id(0)

    def tail_copy(e):
        end = pe_ref[e]
        start = pl.multiple_of(end - rb, rb)
        return pltpu.make_async_copy(zbuf, xs_hbm.at[pl.ds(start, rb), :], sem_z)

    def nonempty(e):
        return pe_ref[e] > (pe_ref[e - 1] if e > 0 else 0)

    def unused_copy(j):
        return pltpu.make_async_copy(zbuf, xs_hbm.at[pl.ds(pl.multiple_of(j * rb, rb), rb), :], sem_z)

    def zero_fill():
        zbuf[...] = jnp.zeros_like(zbuf)
        for e in range(N_EXPERTS):
            pl.when(nonempty(e))(lambda e=e: tail_copy(e).start())
        for e in range(N_EXPERTS):
            pl.when(nonempty(e))(lambda e=e: tail_copy(e).wait())
        first_unused = lax.shift_right_logical(pe_ref[N_EXPERTS - 1], rb.bit_length() - 1)
        lax.fori_loop(first_unused, n_blocks, lambda j, c: (unused_copy(j).start(), c)[1], 0)
        lax.fori_loop(first_unused, n_blocks, lambda j, c: (unused_copy(j).wait(), c)[1], 0)

    if not chained:
        pl.when(i == 0)(zero_fill)

    def copies(r):
        return [_row_copy(h2p_ref, r, xs_hbm, dest_ref[0, 0, k * td + r], sem_r)
                for k in range(TOP_K)]

    for r in range(td):
        for k, cp in enumerate(copies(r)):
            cp.start(priority=k)
    for r in range(td):
        for cp in copies(r):
            cp.wait()


def _dispatch(pad_end, dest_tiles, h2p, n_rows, td, rb, xs_prev=None):
    T = h2p.shape[0]
    chained = xs_prev is not None
    in_specs = [pl.BlockSpec((1, 1, TOP_K * td), lambda i, pe: (i, 0, 0), memory_space=pltpu.SMEM),
                pl.BlockSpec((td, D_MODEL // 2), lambda i, pe: (i, 0))]
    operands = [pad_end, dest_tiles, h2p]
    if chained:
        in_specs.append(pl.BlockSpec(memory_space=pl.ANY))
        operands.append(xs_prev)
    grid_spec = pltpu.PrefetchScalarGridSpec(
        num_scalar_prefetch=1, grid=(T // td,), in_specs=in_specs,
        out_specs=pl.BlockSpec(memory_space=pl.ANY),
        scratch_shapes=[pltpu.VMEM((rb, D_MODEL // 2), jnp.uint32),
                        pltpu.SemaphoreType.DMA(()), pltpu.SemaphoreType.DMA(())],
    )
    return pl.pallas_call(
        functools.partial(_dispatch_kernel, td=td, rb=rb, n_blocks=n_rows // rb, chained=chained),
        grid_spec=grid_spec, out_shape=jax.ShapeDtypeStruct((n_rows, D_MODEL // 2), jnp.uint32),
        input_output_aliases={len(operands) - 1: 0} if chained else {},
        compiler_params=_params(("arbitrary",)), name="dispatch",
    )(*operands)


def _moe_kernel(be_ref, nu_ref, xs_ref, w1_ref, w3_ref, w2_ref, yb_ref, w1b, w3b, w2b):
    i = pl.program_id(0)
    half = D_MODEL // 2

    @pl.when(i >= nu_ref[0])
    def _():
        yb_ref[...] = jnp.zeros_like(yb_ref)

    @pl.when(i < nu_ref[0])
    def _():
        @pl.when((i == 0) | (be_ref[i] != be_ref[jnp.maximum(i - 1, 0)]))
        def _():
            w1b[...] = w1_ref[0].astype(BF16)
            w3b[...] = w3_ref[0].astype(BF16)
            w2b[...] = w2_ref[0].astype(BF16)

        u = xs_ref[...]
        xa = pltpu.bitcast(u & jnp.uint32(0xFFFF0000), F32).astype(BF16)
        xb = pltpu.bitcast(u << 16, F32).astype(BF16)

        def proj(wb):
            return (jnp.dot(xa, wb[0:half, :], preferred_element_type=F32)
                    + jnp.dot(xb, wb[half:, :], preferred_element_type=F32))

        hmid = _silu(proj(w1b)) * proj(w3b)
        yb_ref[...] = jnp.dot(hmid.astype(BF16), w2b[...], preferred_element_type=F32)


def _moe(block_e, n_used, xs, wts, rb):
    n_blocks = block_e.shape[0]

    grid_spec = pltpu.PrefetchScalarGridSpec(
        num_scalar_prefetch=2, grid=(n_blocks,),
        in_specs=[
            pl.BlockSpec((rb, D_MODEL // 2), lambda i, be, nu: (i, 0)),
            pl.BlockSpec((1, D_MODEL, D_EXPERT), lambda i, be, nu: (be[i], 0, 0)),
            pl.BlockSpec((1, D_MODEL, D_EXPERT), lambda i, be, nu: (be[i], 0, 0)),
            pl.BlockSpec((1, D_EXPERT, D_MODEL), lambda i, be, nu: (be[i], 0, 0)),
        ],
        out_specs=pl.BlockSpec((rb, D_MODEL), lambda i, be, nu: (i, 0)),
        scratch_shapes=[pltpu.VMEM((D_MODEL, D_EXPERT), BF16), pltpu.VMEM((D_MODEL, D_EXPERT), BF16),
                        pltpu.VMEM((D_EXPERT, D_MODEL), BF16)],
    )
    return pl.pallas_call(
        _moe_kernel, grid_spec=grid_spec,
        out_shape=jax.ShapeDtypeStruct((n_blocks * rb, D_MODEL), F32),
        compiler_params=_params(("arbitrary",)), name="moe",
    )(block_e, n_used, xs, wts["moe_w1"], wts["moe_w3"], wts["moe_w2"])


def _dispatch_plan(eis, cnt, rb):
    n_tokens = sum(ei.shape[1] for ei in eis)
    n_blocks = -(-(n_tokens * TOP_K + N_EXPERTS * (rb - 1)) // rb)
    counts = cnt[0, :N_EXPERTS].astype(jnp.int32)
    padded = (counts + rb - 1) // rb * rb
    pad_end = jnp.cumsum(padded).astype(jnp.int32)
    pad_start = pad_end - padded
    ei = jnp.concatenate(eis, axis=1)
    e = ei[0:TOP_K, :]
    dest = ei[TOP_K:2 * TOP_K, :]
    for j in range(N_EXPERTS):
        dest = dest + jnp.where(e == j, pad_start[j], 0)
    dests, off = [], 0
    for x in eis:
        dests.append(dest[:, off:off + x.shape[1]])
        off += x.shape[1]
    blk_start = jnp.arange(n_blocks, dtype=jnp.int32) * rb
    block_e = jnp.minimum(jnp.sum(pad_end[None, :] <= blk_start[:, None], axis=1),
                          N_EXPERTS - 1).astype(jnp.int32)
    n_used = (pad_end[N_EXPERTS - 1:] // rb).astype(jnp.int32)
    return pad_end, dests, block_e, n_used, n_blocks * rb


def _dest_tiles(dest, t):
    T = dest.shape[1]
    return dest.reshape(TOP_K, T // t, t).transpose(1, 0, 2).reshape(T // t, 1, TOP_K * t)


def _combine_kernel(dcur_ref, dnext_ref, x1_ref, ew_ref, gf_ref, yb_hbm, y_ref, ybuf, sems,
                    *, tc, n_tiles):
    i = pl.program_id(0)

    def copies(dref, slot, r):
        return [pltpu.make_async_copy(yb_hbm.at[pl.ds(dref[0, 0, k * tc + r], 1), :],
                                      ybuf.at[slot, k, pl.ds(r, 1), :], sems.at[slot])
                for k in range(TOP_K)]

    def gather(dref, slot):
        for r in range(tc):
            for k, cp in enumerate(copies(dref, slot, r)):
                cp.start(priority=k)

    def drain(dref, slot):
        for r in range(tc):
            for cp in copies(dref, slot, r):
                cp.wait()

    pl.when(i == 0)(lambda: gather(dcur_ref, 0))

    def step(slot):
        drain(dcur_ref, slot)
        gather(dnext_ref, 1 - slot)
        w = ew_ref[...]
        moe = w[:, 0:1] * ybuf[slot, 0] + w[:, 1:2] * ybuf[slot, 1]
        x2 = x1_ref[...] + moe
        y_ref[...] = x2 * _rms_scale(x2) * gf_ref[...]
        pl.when(i == n_tiles - 1)(lambda: drain(dnext_ref, 1 - slot))

    pl.when(lax.rem(i, 2) == 0)(lambda: step(0))
    pl.when(lax.rem(i, 2) == 1)(lambda: step(1))


def _combine(dest_tiles, x1, ew, final_g, yb, tc):
    T = x1.shape[0]
    n_tiles = T // tc
    tile = lambda i: (i, 0)
    dspec = lambda f: pl.BlockSpec((1, 1, TOP_K * tc), f, memory_space=pltpu.SMEM)
    return pl.pallas_call(
        functools.partial(_combine_kernel, tc=tc, n_tiles=n_tiles), grid=(n_tiles,),
        in_specs=[dspec(lambda i: (i, 0, 0)),
                  dspec(lambda i: (jnp.minimum(i + 1, n_tiles - 1), 0, 0)),
                  pl.BlockSpec((tc, D_MODEL), tile), pl.BlockSpec((tc, LANES), tile),
                  pl.BlockSpec((1, D_MODEL), lambda i: (0, 0)),
                  pl.BlockSpec(memory_space=pl.ANY)],
        out_specs=pl.BlockSpec((tc, D_MODEL), tile),
        out_shape=jax.ShapeDtypeStruct((T, D_MODEL), F32),
        scratch_shapes=[pltpu.VMEM((2, TOP_K, tc, D_MODEL), F32), pltpu.SemaphoreType.DMA((2,))],
        compiler_params=_params(("arbitrary",)), name="combine",
    )(dest_tiles, dest_tiles, x1, ew, final_g, yb)


def _prep_weights(ln1_g, ln2_g, w_in, lru_conv_w, lru_conv_b, lru_wa, lru_ba, lru_wx, lru_bx,
                  lru_lambda, gdn_conv_w, gdn_a_log, gdn_dt_bias, gdn_norm_g, w_out,
                  router_group_w, router_group_b, router_expert_w, router_expert_b,
                  moe_w1, moe_w3, moe_w2, final_g):
    w_in0 = w_in[0]
    half = LRU_WIDTH // 2
    per_half = half // LRU_BLOCK

    def block_diag(w):
        w4 = w.reshape(2, per_half, LRU_BLOCK, LRU_BLOCK)
        eye = jnp.eye(per_half, dtype=w.dtype)
        return jnp.einsum("snij,nm->snimj", w4, eye).reshape(2, half, half)

    wgate = jnp.concatenate([block_diag(lru_wa[0]), block_diag(lru_wx[0])], axis=2).astype(BF16)
    zeros4 = jnp.zeros((GDN_HEADS,), F32)
    neg_a = jnp.concatenate([zeros4, jnp.exp(gdn_a_log[0].astype(F32))])
    dtb = jnp.concatenate([zeros4, gdn_dt_bias[0].astype(F32)])
    gate_par = jnp.stack([neg_a, dtb])
    w_router = jnp.zeros((D_MODEL, LANES), F32)
    w_router = w_router.at[:, :N_GROUPS].set(router_group_w[0])
    w_router = w_router.at[:, N_GROUPS:N_GROUPS + N_EXPERTS].set(router_expert_w[0])
    b_router = jnp.zeros((1, LANES), F32)
    b_router = b_router.at[0, :N_GROUPS].set(router_group_b[0])
    b_router = b_router.at[0, N_GROUPS:N_GROUPS + N_EXPERTS].set(router_expert_b[0])
    return {
        "ln1_g": ln1_g[0].reshape(1, D_MODEL), "ln2_g": ln2_g[0].reshape(1, D_MODEL),
        "w_main": w_in0.astype(BF16),
        "w_gate_col": w_in0[:, N_MAIN:].astype(BF16),
        "w_gate_row": w_in0[:, N_MAIN:].T.astype(BF16),
        "lru_cw": lru_conv_w[0], "lru_cb": lru_conv_b[0].reshape(1, LRU_WIDTH),
        "lru_wgate": wgate,
        "lru_ba": lru_ba[0].reshape(1, LRU_WIDTH), "lru_bx": lru_bx[0].reshape(1, LRU_WIDTH),
        "lru_lam": lru_lambda[0].reshape(1, LRU_WIDTH),
        "gdn_cw": gdn_conv_w[0], "gate_par_col": gate_par, "gate_par_row": gate_par.T,
        "gdn_ng": gdn_norm_g[0].reshape(1, GDN_DV),
        "w_out": w_out[0].astype(BF16),
        "w_router": w_router, "b_router": b_router,
        "moe_w1": moe_w1[0], "moe_w3": moe_w3[0], "moe_w2": moe_w2[0],
        "final_g": final_g.reshape(1, D_MODEL),
    }


def _pad_conv_state(s):
    return jnp.pad(s, ((0, 0), (SUBLANES - (CONV_W - 1), 0), (0, 0)))


def _mixer(x, conv_l, h_l, conv_g, s_g, cnt0, wts, tl_in, bb, cps, tl_tok):
    B, L, _ = x.shape
    T = B * L
    c = min(CHUNK, L)
    oa, q, k, v, gz, gcol, grow, nlc, nlh, ngc = _mixer_in(
        x, _pad_conv_state(conv_l), h_l.reshape(B, 1, LRU_WIDTH), _pad_conv_state(conv_g), wts, tl_in)
    og, ns = _gdn(q, k, v, gz, gcol, grow, s_g, wts["gdn_ng"], bb, c, cps)
    routed = _mixer_out(oa.reshape(T, LRU_WIDTH), og.reshape(T, GDN_V), x.reshape(T, D_MODEL),
                        cnt0, wts, tl_tok)
    return routed, (nlc[None], nlh.reshape(1, B, LRU_WIDTH), ngc[None], ns[None])


def kernel(x_prompt, x_sample, state_lru_conv, state_lru_h, state_gdn_conv, state_gdn_S, ln1_g, ln2_g, w_in, lru_conv_w, lru_conv_b, lru_wa, lru_ba, lru_wx, lru_bx, lru_lambda, gdn_conv_w, gdn_a_log, gdn_dt_bias, gdn_norm_g, w_out, router_group_w, router_group_b, router_expert_w, router_expert_b, moe_w1, moe_w3, moe_w2, final_g):
    wts = _prep_weights(ln1_g, ln2_g, w_in, lru_conv_w, lru_conv_b, lru_wa, lru_ba, lru_wx, lru_bx,
                        lru_lambda, gdn_conv_w, gdn_a_log, gdn_dt_bias, gdn_norm_g, w_out,
                        router_group_w, router_group_b, router_expert_w, router_expert_b,
                        moe_w1, moe_w3, moe_w2, final_g)
    B, L, _ = x_prompt.shape
    Bs, Ls, _ = x_sample.shape
    dt = x_prompt.dtype
    tile_p, tile_s = 512, Bs * Ls
    row_block = 512
    routed_p, states_p = _mixer(
        x_prompt,
        jnp.zeros((B, CONV_W - 1, LRU_WIDTH), dt), jnp.zeros((B, LRU_WIDTH), dt),
        jnp.zeros((B, CONV_W - 1, GDN_CONV_CH), dt), jnp.zeros((B, GDN_HEADS, GDN_DK, GDN_DV), dt),
        jnp.zeros((1, LANES), F32), wts, tl_in=512, bb=4, cps=2, tl_tok=tile_p)
    x1_p, h2p_p, ew_p, ei_p, cnt_p = routed_p
    routed_s, states_s = _mixer(
        x_sample, state_lru_conv[0], state_lru_h[0], state_gdn_conv[0], state_gdn_S[0],
        cnt_p, wts, tl_in=Ls, bb=2, cps=1, tl_tok=tile_s)
    x1_s, h2p_s, ew_s, ei_s, cnt = routed_s

    pad_end, (dest_p, dest_s), block_e, n_used, n_rows = _dispatch_plan([ei_p, ei_s], cnt, row_block)
    xs = _dispatch(pad_end, _dest_tiles(dest_p, tile_p), h2p_p, n_rows, tile_p, row_block)
    xs = _dispatch(pad_end, _dest_tiles(dest_s, tile_s), h2p_s, n_rows, tile_s, row_block, xs_prev=xs)
    yb = _moe(block_e, n_used, xs, wts, row_block)
    y_p = _combine(_dest_tiles(dest_p, tile_p), x1_p, ew_p, wts["final_g"], yb, tile_p)
    y_s = _combine(_dest_tiles(dest_s, tile_s), x1_s, ew_s, wts["final_g"], yb, tile_s)
    return (y_p.reshape(B, L, D_MODEL), y_s.reshape(Bs, Ls, D_MODEL)) + states_p + states_s
```

```python
import functools

import jax
import jax.numpy as jnp
from jax import lax
from jax.experimental import pallas as pl
from jax.experimental.pallas import tpu as pltpu

F32 = jnp.float32
BF16 = jnp.bfloat16

D_MODEL = 1024
CONV_W = 4
LRU_WIDTH = 512
LRU_BLOCKS = 8
LRU_BLOCK = LRU_WIDTH // LRU_BLOCKS
LRU_C = 8.0
GDN_HEADS = 4
GDN_DK = 128
GDN_DV = 128
GDN_QK = GDN_HEADS * GDN_DK
GDN_V = GDN_HEADS * GDN_DV
GDN_CONV_CH = 2 * GDN_QK + GDN_V
N_GATE = 2 * GDN_HEADS
N_MAIN = 2 * LRU_WIDTH + GDN_CONV_CH + GDN_V
CHUNK = 64
N_GROUPS = 4
EXPERTS_PER_GROUP = 8
N_EXPERTS = N_GROUPS * EXPERTS_PER_GROUP
TOP_K = 2
D_EXPERT = 256
EPS = 1e-6

SUBLANES = 8
LANES = 128
MXU_DIM = 256
VMEM_LIMIT_BYTES = 48 * 1024 * 1024


def _bdot(a, b):
    return jnp.dot(a.astype(BF16), b.astype(BF16), preferred_element_type=F32)


def _bdot_nt(a, b):
    return lax.dot_general(a.astype(BF16), b.astype(BF16), (((1,), (1,)), ((), ())),
                           preferred_element_type=F32)


def _bdot_tn(a, b):
    return lax.dot_general(a.astype(BF16), b.astype(BF16), (((0,), (0,)), ((), ())),
                           preferred_element_type=F32)


def _split2(a):
    hi = a.astype(BF16)
    lo = (a - hi.astype(F32)).astype(BF16)
    return hi, lo


def _dot3(a, b):
    ah, al = _split2(a)
    bh, bl = _split2(b)
    d = functools.partial(jnp.dot, preferred_element_type=F32)
    if 3 * a.shape[1] <= MXU_DIM:
        return d(jnp.concatenate([ah, ah, al], axis=1), jnp.concatenate([bh, bl, bh], axis=0))
    return d(ah, bh) + (d(ah, bl) + d(al, bh))


def _dot_exact_rhs(a, b01):
    b = b01.astype(BF16)
    a1 = a.astype(BF16)
    r1 = a - a1.astype(F32)
    a2 = r1.astype(BF16)
    a3 = (r1 - a2.astype(F32)).astype(BF16)
    d = functools.partial(jnp.dot, preferred_element_type=F32)
    return d(a1, b) + (d(a2, b) + d(a3, b))


def _dot_exact_lhs(a01, b):
    a = a01.astype(BF16)
    b1 = b.astype(BF16)
    r1 = b - b1.astype(F32)
    b2 = r1.astype(BF16)
    b3 = (r1 - b2.astype(F32)).astype(BF16)
    d = functools.partial(jnp.dot, preferred_element_type=F32)
    return d(a, b1) + (d(a, b2) + d(a, b3))


NEG_LOG2_E = -1.4426950408889634


def _sigmoid(x):
    return 1.0 / (1.0 + jnp.exp2(x * NEG_LOG2_E))


def _softplus(x):
    return jnp.maximum(x, 0.0) + jnp.log(1.0 + jnp.exp(-jnp.abs(x)))


def _silu(x):
    return x * _sigmoid(x)


def _gelu_tanh(x):
    c = 0.7978845608028654
    return 0.5 * x * (1.0 + jnp.tanh(c * (x + 0.044715 * (x * x * x))))


def _rms_scale(x):
    return lax.rsqrt(jnp.mean(x * x, axis=-1, keepdims=True) + EPS)


def _params(sem):
    return pltpu.CompilerParams(dimension_semantics=sem, vmem_limit_bytes=VMEM_LIMIT_BYTES)


def _mixer_in_kernel(x_ref, g1_ref, wmain_ref, wgc_ref, wgr_ref,
                     lcw_ref, lcb_ref, wgate_ref, ba_ref, bx_ref, lam_ref,
                     gcw_ref, gpc_ref, gpr_ref,
                     lconv0_ref, lh0_ref, gconv0_ref,
                     outa_ref, q_ref, k_ref, v_ref, gz_ref, gcol_ref, grow_ref,
                     nlc_ref, nlh_ref, ngc_ref,
                     xl_buf, xg_buf, hcar, acum, bcum, *, tl):
    lt = pl.program_id(1)
    s1, s2, s3 = LRU_WIDTH, 2 * LRU_WIDTH, 2 * LRU_WIDTH + GDN_CONV_CH

    @pl.when(lt == 0)
    def _():
        xl_buf[0:SUBLANES, :] = lconv0_ref[0]
        xg_buf[0:SUBLANES, :] = gconv0_ref[0]
        hcar[...] = lh0_ref[0]

    @pl.when(lt > 0)
    def _():
        xl_buf[0:SUBLANES, :] = xl_buf[tl:tl + SUBLANES, :]
        xg_buf[0:SUBLANES, :] = xg_buf[tl:tl + SUBLANES, :]

    x = x_ref[0]
    hn = (x * _rms_scale(x) * g1_ref[...]).astype(BF16)

    xl_buf[SUBLANES:SUBLANES + tl, :] = jnp.dot(hn, wmain_ref[:, 0:s1], preferred_element_type=F32)
    base = SUBLANES - (CONV_W - 1)

    def causal_conv(buf, w_ref):
        xall = buf[...]
        acc = None
        for j in range(CONV_W):
            back = CONV_W - 1 - j
            shifted = pltpu.roll(xall, back, axis=0) if back else xall
            term = shifted[SUBLANES:, :] * w_ref[j:j + 1, :]
            acc = term if acc is None else acc + term
        return acc

    xc = causal_conv(xl_buf, lcw_ref) + lcb_ref[...]
    nlc_ref[0] = xl_buf[tl + base:tl + SUBLANES, :]

    half = LRU_WIDTH // 2
    pre = [_bdot(xc[:, s * half:(s + 1) * half], wgate_ref[s]) for s in range(2)]
    r_pre = jnp.concatenate([pre[0][:, :half], pre[1][:, :half]], axis=1)
    i_pre = jnp.concatenate([pre[0][:, half:], pre[1][:, half:]], axis=1)
    r = _sigmoid(r_pre + ba_ref[...])
    ig = _sigmoid(i_pre + bx_ref[...])
    a = jnp.exp2(r * (_softplus(-lam_ref[...]) * (LRU_C * NEG_LOG2_E)))
    b = jnp.sqrt(1.0 - a * a) * (ig * xc)

    ng = tl // SUBLANES
    a3 = a.reshape(ng, SUBLANES, LRU_WIDTH)
    b3 = b.reshape(ng, SUBLANES, LRU_WIDTH)
    row = lax.broadcasted_iota(jnp.int32, (ng, SUBLANES, LRU_WIDTH), 1)
    sh = 1
    while sh < SUBLANES:
        a_s = pltpu.roll(a3, sh, axis=1)
        b_s = pltpu.roll(b3, sh, axis=1)
        m = row >= sh
        b3 = jnp.where(m, a3 * b_s + b3, b3)
        a3 = jnp.where(m, a3 * a_s, a3)
        sh *= 2
    h0 = hcar[...]
    if ng % SUBLANES:
        parts = []
        hlast = h0
        for g in range(ng):
            hg = a3[g] * hlast + b3[g]
            parts.append(hg)
            hlast = hg[SUBLANES - 1:SUBLANES, :]
        hs3 = jnp.stack(parts)
    else:
        def group_last(scr, val3):
            val = val3.reshape(tl, LRU_WIDTH)
            nch = LRU_WIDTH // LANES
            for ch in range(nch):
                scr[ch] = val[:, ch * LANES:(ch + 1) * LANES]
            return jnp.concatenate(
                [scr[ch, pl.ds(SUBLANES - 1, ng, stride=SUBLANES), :] for ch in range(nch)], axis=1)

        ag = group_last(acum, a3)
        bg = group_last(bcum, b3)
        grow_i = lax.broadcasted_iota(jnp.int32, (ng, LRU_WIDTH), 0)
        sh = 1
        while sh < ng:
            a_s = pltpu.roll(ag, sh, axis=0)
            b_s = pltpu.roll(bg, sh, axis=0)
            m = grow_i >= sh
            bg = jnp.where(m, ag * b_s + bg, bg)
            ag = jnp.where(m, ag * a_s, ag)
            sh *= 2
        hend = ag * h0 + bg
        hin = jnp.where(grow_i == 0, h0, pltpu.roll(hend, 1, axis=0))
        hs3 = a3 * hin.reshape(ng, 1, LRU_WIDTH) + b3
        hlast = hend[ng - 1:ng, :]
    hcar[...] = hlast
    nlh_ref[0] = hlast
    y = jnp.dot(hn, wmain_ref[:, s1:s2], preferred_element_type=F32)
    outa_ref[0] = (hs3.reshape(tl, LRU_WIDTH) * _gelu_tanh(y)).astype(BF16)

    xg_buf[SUBLANES:SUBLANES + tl, :] = jnp.dot(hn, wmain_ref[:, s2:s3], preferred_element_type=F32)
    ngc_ref[0] = xg_buf[tl + base:tl + SUBLANES, :]
    qkv = _silu(causal_conv(xg_buf, gcw_ref))
    for h in range(GDN_HEADS):
        qh = qkv[:, h * GDN_DK:(h + 1) * GDN_DK]
        kh = qkv[:, GDN_QK + h * GDN_DK:GDN_QK + (h + 1) * GDN_DK]
        qs = lax.rsqrt(jnp.sum(qh * qh, axis=-1, keepdims=True) + EPS) * (GDN_DK ** -0.5)
        ks = lax.rsqrt(jnp.sum(kh * kh, axis=-1, keepdims=True) + EPS)
        q_ref[0, :, h * GDN_DK:(h + 1) * GDN_DK] = qh * qs
        k_ref[0, :, h * GDN_DK:(h + 1) * GDN_DK] = kh * ks
    v_ref[0] = qkv[:, 2 * GDN_QK:]
    gz_ref[0] = _silu(jnp.dot(hn, wmain_ref[:, s3:], preferred_element_type=F32))

    gc = jnp.dot(hn, wgc_ref[...], preferred_element_type=F32)
    lane = lax.broadcasted_iota(jnp.int32, gc.shape, 1)
    gcol_ref[0] = jnp.where(lane < GDN_HEADS, _sigmoid(gc),
                            -gpc_ref[0:1, :] * _softplus(gc + gpc_ref[1:2, :]))
    gr = lax.dot_general(wgr_ref[...], hn, (((1,), (1,)), ((), ())), preferred_element_type=F32)
    sub = lax.broadcasted_iota(jnp.int32, gr.shape, 0)
    grow_ref[0] = jnp.where(sub < GDN_HEADS, _sigmoid(gr),
                            -gpr_ref[:, 0:1] * _softplus(gr + gpr_ref[:, 1:2]))


def _mixer_in(x, conv_l8, h_l, conv_g8, wts, tl):
    B, L, _ = x.shape
    nl = L // tl
    const2 = lambda b, l: (0, 0)
    const3 = lambda b, l: (0, 0, 0)
    per_b3 = lambda b, l: (b, 0, 0)
    tile3 = lambda b, l: (b, l, 0)
    in_specs = [
        pl.BlockSpec((1, tl, D_MODEL), tile3),
        pl.BlockSpec((1, D_MODEL), const2),
        pl.BlockSpec((D_MODEL, N_MAIN), const2),
        pl.BlockSpec((D_MODEL, N_GATE), const2),
        pl.BlockSpec((N_GATE, D_MODEL), const2),
        pl.BlockSpec((CONV_W, LRU_WIDTH), const2),
        pl.BlockSpec((1, LRU_WIDTH), const2),
        pl.BlockSpec((2, LRU_WIDTH // 2, LRU_WIDTH), const3),
        pl.BlockSpec((1, LRU_WIDTH), const2),
        pl.BlockSpec((1, LRU_WIDTH), const2),
        pl.BlockSpec((1, LRU_WIDTH), const2),
        pl.BlockSpec((CONV_W, GDN_CONV_CH), const2),
        pl.BlockSpec((2, N_GATE), const2),
        pl.BlockSpec((N_GATE, 2), const2),
        pl.BlockSpec((1, SUBLANES, LRU_WIDTH), per_b3),
        pl.BlockSpec((1, 1, LRU_WIDTH), per_b3),
        pl.BlockSpec((1, SUBLANES, GDN_CONV_CH), per_b3),
    ]
    out_shape = (
        jax.ShapeDtypeStruct((B, L, LRU_WIDTH), BF16),
        jax.ShapeDtypeStruct((B, L, GDN_QK), F32),
        jax.ShapeDtypeStruct((B, L, GDN_QK), F32),
        jax.ShapeDtypeStruct((B, L, GDN_V), F32),
        jax.ShapeDtypeStruct((B, L, GDN_V), F32),
        jax.ShapeDtypeStruct((B, L, N_GATE), F32),
        jax.ShapeDtypeStruct((B, N_GATE, L), F32),
        jax.ShapeDtypeStruct((B, CONV_W - 1, LRU_WIDTH), F32),
        jax.ShapeDtypeStruct((B, 1, LRU_WIDTH), F32),
        jax.ShapeDtypeStruct((B, CONV_W - 1, GDN_CONV_CH), F32),
    )
    out_specs = (
        pl.BlockSpec((1, tl, LRU_WIDTH), tile3),
        pl.BlockSpec((1, tl, GDN_QK), tile3),
        pl.BlockSpec((1, tl, GDN_QK), tile3),
        pl.BlockSpec((1, tl, GDN_V), tile3),
        pl.BlockSpec((1, tl, GDN_V), tile3),
        pl.BlockSpec((1, tl, N_GATE), tile3),
        pl.BlockSpec((1, N_GATE, tl), lambda b, l: (b, 0, l)),
        pl.BlockSpec((1, CONV_W - 1, LRU_WIDTH), per_b3),
        pl.BlockSpec((1, 1, LRU_WIDTH), per_b3),
        pl.BlockSpec((1, CONV_W - 1, GDN_CONV_CH), per_b3),
    )
    scratch = [
        pltpu.VMEM((tl + SUBLANES, LRU_WIDTH), F32),
        pltpu.VMEM((tl + SUBLANES, GDN_CONV_CH), F32),
        pltpu.VMEM((1, LRU_WIDTH), F32),
        pltpu.VMEM((LRU_WIDTH // LANES, tl, LANES), F32),
        pltpu.VMEM((LRU_WIDTH // LANES, tl, LANES), F32),
    ]
    return pl.pallas_call(
        functools.partial(_mixer_in_kernel, tl=tl),
        grid=(B, nl), in_specs=in_specs, out_specs=out_specs, out_shape=out_shape,
        scratch_shapes=scratch, compiler_params=_params(("arbitrary", "arbitrary")),
        name="mixer_in",
    )(x, wts["ln1_g"], wts["w_main"], wts["w_gate_col"], wts["w_gate_row"],
      wts["lru_cw"], wts["lru_cb"], wts["lru_wgate"], wts["lru_ba"], wts["lru_bx"], wts["lru_lam"],
      wts["gdn_cw"], wts["gate_par_col"], wts["gate_par_row"],
      conv_l8, h_l, conv_g8)


def _tri_inverse_all(lms, c):
    r = lax.broadcasted_iota(jnp.int32, (c, LANES), 0)
    lane = lax.broadcasted_iota(jnp.int32, (c, LANES), 1)
    col = jnp.bitwise_and(lane, c - 1)
    eye = (r == col).astype(F32)
    odd_block = jnp.bitwise_and(jnp.right_shift(lane, c.bit_length() - 1), 1) == 1

    def same_block(n):
        sh = n.bit_length() - 1
        return jnp.right_shift(r, sh) == jnp.right_shift(col, sh)

    def pieces(a, b):
        ah = a.astype(BF16)
        ah32 = ah.astype(F32)
        mixed = jnp.where(odd_block, a - ah32, ah32).astype(BF16)
        lhs = (mixed if 3 * c <= LANES else jnp.concatenate([mixed, ah], axis=1))[:, :3 * c]
        bh = b.astype(BF16)
        bl = (b - bh.astype(F32)).astype(BF16)
        return lhs, jnp.concatenate([bh, bh, bl], axis=0)

    def dot3_all(xs, ys):
        out = []
        for i in range(0, len(xs), 2):
            (l0, r0), (l1, r1) = pieces(xs[i], ys[i]), pieces(xs[i + 1], ys[i + 1])
            both = jnp.dot(jnp.concatenate([l0, l1], axis=0), jnp.concatenate([r0, r1], axis=1),
                           preferred_element_type=F32)
            out += [both[:c, :LANES], both[c:, LANES:]]
        return out

    blk8 = same_block(SUBLANES)
    l8 = [jnp.where(blk8, lm, 0.0) for lm in lms]
    l2 = dot3_all(l8, l8)
    n1 = [eye - a for a in l8]
    p = [a + b for a, b in zip(n1, dot3_all(n1, l2))]
    l4 = dot3_all(l2, l2)
    p = [a + b for a, b in zip(p, dot3_all(p, l4))]
    n = SUBLANES
    while n < c:
        off = same_block(2 * n) & jnp.logical_not(same_block(n))
        lo = [jnp.where(off, lm, 0.0) for lm in lms]
        t = dot3_all(p, lo)
        p = [a - b for a, b in zip(p, dot3_all(t, p))]
        n *= 2
    return p


def _gdn_kernel(q_ref, k_ref, v_ref, gz_ref, gcol_ref, grow_ref, s0_ref, ng_ref,
                o_ref, sout_ref, s_scr, *, bb, c, cps):
    ct = pl.program_id(1)

    @pl.when(ct == 0)
    def _():
        s_scr[...] = s0_ref[...]

    reps = LANES // c
    r = lax.broadcasted_iota(jnp.int32, (c, LANES), 0)
    col = jnp.bitwise_and(lax.broadcasted_iota(jnp.int32, (c, LANES), 1), c - 1)
    lower = r >= col
    strict = r > col
    rs = lax.broadcasted_iota(jnp.int32, (c, c), 0)
    cs = lax.broadcasted_iota(jnp.int32, (c, c), 1)
    tri_l = (rs >= cs).astype(F32)
    tri_u = jnp.where(r <= col, 1.0, 0.0)

    def hs(h):
        return slice(h * GDN_DK, (h + 1) * GDN_DK)

    def rows(j):
        return slice(j * c, (j + 1) * c)

    chains = [(j, b, h) for j in range(cps) for b in range(bb) for h in range(GDN_HEADS)]
    pairs = [(j, b) for j in range(cps) for b in range(bb)]
    gates_c = {jb: gcol_ref[jb[1], rows(jb[0]), :] for jb in pairs}
    gcum_c = {jb: _dot_exact_lhs(tri_l, gates_c[jb]) for jb in pairs}
    gcum_r = {jb: _dot_exact_rhs(grow_ref[jb[1], jb[0]], tri_u) for jb in pairs}
    beta = [gates_c[(j, b)][:, h:h + 1] for j, b, h in chains]
    gc_c = [gcum_c[(j, b)][:, GDN_HEADS + h:GDN_HEADS + h + 1] for j, b, h in chains]
    gc_r = [gcum_r[(j, b)][GDN_HEADS + h:GDN_HEADS + h + 1, :] for j, b, h in chains]
    decay = [jnp.where(lower, jnp.exp(jnp.where(lower, a - b, 0.0)), 0.0) for a, b in zip(gc_c, gc_r)]
    e_c = [jnp.exp(a) for a in gc_c]
    kb = [k_ref[b, rows(j), hs(h)] * bt for (j, b, h), bt in zip(chains, beta)]
    kk = [_bdot_nt(jnp.concatenate([a, q_ref[b, rows(j), hs(h)]], axis=0),
                   jnp.concatenate([k_ref[b, rows(j), hs(h)]] * reps, axis=0))
          for (j, b, h), a in zip(chains, kb)]
    lmat = [jnp.where(strict, x[:c] * d, 0.0) for x, d in zip(kk, decay)]
    attn = [x[c:, :c] * d[:, :c] for x, d in zip(kk, decay)]
    t_inv = _tri_inverse_all(lmat, c)
    uw = [_bdot(t[:, :c], jnp.concatenate([v_ref[b, rows(j), hs(h)] * bt, a * e], axis=1))
          for (j, b, h), t, bt, a, e in zip(chains, t_inv, beta, kb, e_c)]
    per = bb * GDN_HEADS
    for j in range(cps):
        sel = range(j * per, (j + 1) * per)
        ws = [_bdot(jnp.concatenate([uw[i][:, GDN_DV:],
                                     q_ref[chains[i][1], rows(j), hs(chains[i][2])] * e_c[i]], axis=0),
                    s_scr[chains[i][1], chains[i][2]]) for i in sel]
        v_new = [uw[i][:, :GDN_DV] - x[:c] for i, x in zip(sel, ws)]
        o = [x[c:] + _bdot(attn[i], vn) for i, x, vn in zip(sel, ws, v_new)]
        for i, vn in zip(sel, v_new):
            _, b, h = chains[i]
            g_last = gc_c[i][c - 1:c, :]
            k_dec = k_ref[b, rows(j), hs(h)] * jnp.exp(g_last - gc_c[i])
            s_scr[b, h] = s_scr[b, h] * jnp.exp(g_last) + _bdot_tn(k_dec, vn)
        for i, x in zip(sel, o):
            _, b, h = chains[i]
            on = x * _rms_scale(x) * ng_ref[...]
            o_ref[b, rows(j), hs(h)] = (on * gz_ref[b, rows(j), hs(h)]).astype(BF16)

    sout_ref[...] = s_scr[...]


def _gdn(q, k, v, gz, gcol, grow, s0, ng, bb, c, cps):
    B, L, _ = q.shape
    nc = L // c
    tl = cps * c
    tile = lambda i, t: (i, t, 0)
    per_b4 = lambda i, t: (i, 0, 0, 0)
    in_specs = [
        pl.BlockSpec((bb, tl, GDN_QK), tile),
        pl.BlockSpec((bb, tl, GDN_QK), tile),
        pl.BlockSpec((bb, tl, GDN_V), tile),
        pl.BlockSpec((bb, tl, GDN_V), tile),
        pl.BlockSpec((bb, tl, N_GATE), tile),
        pl.BlockSpec((bb, cps, N_GATE, c), lambda i, t: (i, t, 0, 0)),
        pl.BlockSpec((bb, GDN_HEADS, GDN_DK, GDN_DV), per_b4),
        pl.BlockSpec((1, GDN_DV), lambda i, t: (0, 0)),
    ]
    out_shape = (jax.ShapeDtypeStruct((B, L, GDN_V), BF16),
                 jax.ShapeDtypeStruct((B, GDN_HEADS, GDN_DK, GDN_DV), F32))
    out_specs = (pl.BlockSpec((bb, tl, GDN_V), tile),
                 pl.BlockSpec((bb, GDN_HEADS, GDN_DK, GDN_DV), per_b4))
    return pl.pallas_call(
        functools.partial(_gdn_kernel, bb=bb, c=c, cps=cps),
        grid=(B // bb, nc // cps), in_specs=in_specs, out_specs=out_specs, out_shape=out_shape,
        scratch_shapes=[pltpu.VMEM((bb, GDN_HEADS, GDN_DK, GDN_DV), F32)],
        compiler_params=_params(("arbitrary", "arbitrary")),
        name="gdn",
    )(q, k, v, gz, gcol, grow.reshape(B, N_GATE, nc, c).transpose(0, 2, 1, 3), s0, ng)


def _mixer_out_kernel(oa_ref, og_ref, x_ref, wout_ref, g2_ref, wr_ref, br_ref, cnt0_ref,
                      x1_ref, h2p_ref, ew_ref, ei_ref, cnt_ref, carry, *, tl):
    @pl.when(pl.program_id(0) == 0)
    def _():
        carry[...] = cnt0_ref[...]

    mix = (jnp.dot(oa_ref[...], wout_ref[0:LRU_WIDTH, :], preferred_element_type=F32)
           + jnp.dot(og_ref[...], wout_ref[LRU_WIDTH:, :], preferred_element_type=F32))
    x1 = x_ref[...] + mix
    x1_ref[...] = x1
    h2 = x1 * _rms_scale(x1) * g2_ref[...]
    bits = pltpu.bitcast(h2.astype(BF16).astype(F32), jnp.uint32)
    h2p_ref[...] = (bits[:, :D_MODEL // 2] & jnp.uint32(0xFFFF0000)) | (bits[:, D_MODEL // 2:] >> 16)
    logits = _bdot(h2, wr_ref[...]) + br_ref[...]
    lane = lax.broadcasted_iota(jnp.int32, logits.shape, 1)
    lane_f = lane.astype(F32)
    neg = jnp.float32(-jnp.inf)
    big = jnp.float32(1e9)
    gmask = lane < N_GROUPS
    gl = jnp.where(gmask, logits, neg)
    gmax = jnp.max(gl, axis=1, keepdims=True)
    grp = jnp.min(jnp.where(gl == gmax, lane_f, big), axis=1, keepdims=True)
    p_grp = 1.0 / jnp.sum(jnp.where(gmask, jnp.exp(gl - gmax), 0.0), axis=1, keepdims=True)
    egrp = jnp.right_shift(lane - N_GROUPS, EXPERTS_PER_GROUP.bit_length() - 1).astype(F32)
    sel = (lane >= N_GROUPS) & (lane < N_GROUPS + N_EXPERTS) & (egrp == grp)
    el = jnp.where(sel, logits, neg)
    v1 = jnp.max(el, axis=1, keepdims=True)
    i1 = jnp.min(jnp.where(el == v1, lane_f, big), axis=1, keepdims=True)
    el2 = jnp.where(lane_f == i1, neg, el)
    v2 = jnp.max(el2, axis=1, keepdims=True)
    i2 = jnp.min(jnp.where(el2 == v2, lane_f, big), axis=1, keepdims=True)
    e2 = jnp.exp(v2 - v1)
    den = 1.0 / (1.0 + e2)
    w1 = den * p_grp
    w2 = (e2 * den) * p_grp
    ew_ref[...] = jnp.where(lane == 0, w1, jnp.where(lane == 1, w2, 0.0))

    ex1 = i1 - N_GROUPS
    ex2 = i2 - N_GROUPS
    hit1 = lane_f == ex1
    hit2 = lane_f == ex2
    oh = jnp.concatenate([jnp.where(hit1, 1.0, 0.0), jnp.where(hit2, 1.0, 0.0)], axis=1)
    rr = lax.broadcasted_iota(jnp.int32, (tl, tl), 0)
    cc = lax.broadcasted_iota(jnp.int32, (tl, tl), 1)
    before = jnp.dot(jnp.where(rr > cc, 1.0, 0.0).astype(BF16), oh.astype(BF16),
                     preferred_element_type=F32)
    tot = jnp.sum(oh, axis=0, keepdims=True)
    tot1, tot2 = tot[:, :LANES], tot[:, LANES:]
    c0 = carry[...]
    rank1 = jnp.sum(jnp.where(hit1, c0 + before[:, :LANES], 0.0), axis=1, keepdims=True)
    rank2 = jnp.sum(jnp.where(hit2, (c0 + tot1) + before[:, LANES:], 0.0), axis=1, keepdims=True)
    c1 = c0 + (tot1 + tot2)
    carry[...] = c1
    cnt_ref[...] = c1
    idx = jnp.where(lane == 0, ex1, jnp.where(lane == 1, ex2, jnp.where(
        lane == 2, rank1, jnp.where(lane == 3, rank2, 0.0))))
    ei_ref[...] = jnp.transpose(idx)[0:SUBLANES, :].astype(jnp.int32)


def _mixer_out(oa, og, x, cnt0, wts, tl):
    T = x.shape[0]
    tile = lambda i: (i, 0)
    const = lambda i: (0, 0)
    in_specs = [
        pl.BlockSpec((tl, LRU_WIDTH), tile),
        pl.BlockSpec((tl, GDN_V), tile),
        pl.BlockSpec((tl, D_MODEL), tile),
        pl.BlockSpec((D_MODEL, D_MODEL), const),
        pl.BlockSpec((1, D_MODEL), const),
        pl.BlockSpec((D_MODEL, LANES), const),
        pl.BlockSpec((1, LANES), const),
        pl.BlockSpec((1, LANES), const),
    ]
    out_shape = (jax.ShapeDtypeStruct((T, D_MODEL), F32),
                 jax.ShapeDtypeStruct((T, D_MODEL // 2), jnp.uint32),
                 jax.ShapeDtypeStruct((T, LANES), F32),
                 jax.ShapeDtypeStruct((SUBLANES, T), jnp.int32),
                 jax.ShapeDtypeStruct((1, LANES), F32))
    out_specs = (pl.BlockSpec((tl, D_MODEL), tile), pl.BlockSpec((tl, D_MODEL // 2), tile),
                 pl.BlockSpec((tl, LANES), tile), pl.BlockSpec((SUBLANES, tl), lambda i: (0, i)),
                 pl.BlockSpec((1, LANES), const))
    return pl.pallas_call(
        functools.partial(_mixer_out_kernel, tl=tl), grid=(T // tl,), in_specs=in_specs,
        out_specs=out_specs, out_shape=out_shape, scratch_shapes=[pltpu.VMEM((1, LANES), F32)],
        compiler_params=_params(("arbitrary",)), name="mixer_out",
    )(oa, og, x, wts["w_out"], wts["ln2_g"], wts["w_router"], wts["b_router"], cnt0)


def _row_copy(src, src_row, dst, dst_row, sem):
    return pltpu.make_async_copy(src.at[pl.ds(src_row, 1), :], dst.at[pl.ds(dst_row, 1), :], sem)


def _dispatch_kernel(pe_ref, dest_ref, h2p_ref, *rest, td, rb, n_blocks, chained):
    xs_hbm, zbuf, sem_z, sem_r = rest[1:] if chained else rest
    i = pl.program_id(0)

    def tail_copy(e):
        end = pe_ref[e]
        start = pl.multiple_of(end - rb, rb)
        return pltpu.make_async_copy(zbuf, xs_hbm.at[pl.ds(start, rb), :], sem_z)

    def nonempty(e):
        return pe_ref[e] > (pe_ref[e - 1] if e > 0 else 0)

    def unused_copy(j):
        return pltpu.make_async_copy(zbuf, xs_hbm.at[pl.ds(pl.multiple_of(j * rb, rb), rb), :], sem_z)

    def zero_fill():
        zbuf[...] = jnp.zeros_like(zbuf)
        for e in range(N_EXPERTS):
            pl.when(nonempty(e))(lambda e=e: tail_copy(e).start())
        for e in range(N_EXPERTS):
            pl.when(nonempty(e))(lambda e=e: tail_copy(e).wait())
        first_unused = lax.shift_right_logical(pe_ref[N_EXPERTS - 1], rb.bit_length() - 1)
        lax.fori_loop(first_unused, n_blocks, lambda j, c: (unused_copy(j).start(), c)[1], 0)
        lax.fori_loop(first_unused, n_blocks, lambda j, c: (unused_copy(j).wait(), c)[1], 0)

    if not chained:
        pl.when(i == 0)(zero_fill)

    def copies(r):
        return [_row_copy(h2p_ref, r, xs_hbm, dest_ref[0, 0, k * td + r], sem_r)
                for k in range(TOP_K)]

    for r in range(td):
        for k, cp in enumerate(copies(r)):
            cp.start(priority=k)
    for r in range(td):
        for cp in copies(r):
            cp.wait()


def _dispatch(pad_end, dest_tiles, h2p, n_rows, td, rb, xs_prev=None):
    T = h2p.shape[0]
    chained = xs_prev is not None
    in_specs = [pl.BlockSpec((1, 1, TOP_K * td), lambda i, pe: (i, 0, 0), memory_space=pltpu.SMEM),
                pl.BlockSpec((td, D_MODEL // 2), lambda i, pe: (i, 0))]
    operands = [pad_end, dest_tiles, h2p]
    if chained:
        in_specs.append(pl.BlockSpec(memory_space=pl.ANY))
        operands.append(xs_prev)
    grid_spec = pltpu.PrefetchScalarGridSpec(
        num_scalar_prefetch=1, grid=(T // td,), in_specs=in_specs,
        out_specs=pl.BlockSpec(memory_space=pl.ANY),
        scratch_shapes=[pltpu.VMEM((rb, D_MODEL // 2), jnp.uint32),
                        pltpu.SemaphoreType.DMA(()), pltpu.SemaphoreType.DMA(())],
    )
    return pl.pallas_call(
        functools.partial(_dispatch_kernel, td=td, rb=rb, n_blocks=n_rows // rb, chained=chained),
        grid_spec=grid_spec, out_shape=jax.ShapeDtypeStruct((n_rows, D_MODEL // 2), jnp.uint32),
        input_output_aliases={len(operands) - 1: 0} if chained else {},
        compiler_params=_params(("arbitrary",)), name="dispatch",
    )(*operands)


def _moe_kernel(be_ref, nu_ref, xs_ref, w1_ref, w3_ref, w2_ref, yb_ref):
    del be_ref
    half = D_MODEL // 2

    @pl.when(pl.program_id(0) >= nu_ref[0])
    def _():
        yb_ref[...] = jnp.zeros_like(yb_ref)

    @pl.when(pl.program_id(0) < nu_ref[0])
    def _():
        u = xs_ref[...]
        xa = pltpu.bitcast(u & jnp.uint32(0xFFFF0000), F32).astype(BF16)
        xb = pltpu.bitcast(u << 16, F32).astype(BF16)

        def proj(w_ref):
            return (jnp.dot(xa, w_ref[0, 0:half, :].astype(BF16), preferred_element_type=F32)
                    + jnp.dot(xb, w_ref[0, half:, :].astype(BF16), preferred_element_type=F32))

        hmid = _silu(proj(w1_ref)) * proj(w3_ref)
        yb_ref[...] = _bdot(hmid, w2_ref[0])


def _moe(block_e, n_used, xs, wts, rb):
    n_blocks = block_e.shape[0]

    grid_spec = pltpu.PrefetchScalarGridSpec(
        num_scalar_prefetch=2, grid=(n_blocks,),
        in_specs=[
            pl.BlockSpec((rb, D_MODEL // 2), lambda i, be, nu: (i, 0)),
            pl.BlockSpec((1, D_MODEL, D_EXPERT), lambda i, be, nu: (be[i], 0, 0)),
            pl.BlockSpec((1, D_MODEL, D_EXPERT), lambda i, be, nu: (be[i], 0, 0)),
            pl.BlockSpec((1, D_EXPERT, D_MODEL), lambda i, be, nu: (be[i], 0, 0)),
        ],
        out_specs=pl.BlockSpec((rb, D_MODEL), lambda i, be, nu: (i, 0)),
    )
    return pl.pallas_call(
        _moe_kernel, grid_spec=grid_spec,
        out_shape=jax.ShapeDtypeStruct((n_blocks * rb, D_MODEL), F32),
        compiler_params=_params(("arbitrary",)), name="moe",
    )(block_e, n_used, xs, wts["moe_w1"], wts["moe_w3"], wts["moe_w2"])


def _dispatch_plan(eis, cnt, rb):
    n_tokens = sum(ei.shape[1] for ei in eis)
    n_blocks = -(-(n_tokens * TOP_K + N_EXPERTS * (rb - 1)) // rb)
    counts = cnt[0, :N_EXPERTS].astype(jnp.int32)
    padded = (counts + rb - 1) // rb * rb
    pad_end = jnp.cumsum(padded).astype(jnp.int32)
    pad_start = pad_end - padded
    ei = jnp.concatenate(eis, axis=1)
    e = ei[0:TOP_K, :]
    dest = ei[TOP_K:2 * TOP_K, :]
    for j in range(N_EXPERTS):
        dest = dest + jnp.where(e == j, pad_start[j], 0)
    dests, off = [], 0
    for x in eis:
        dests.append(dest[:, off:off + x.shape[1]])
        off += x.shape[1]
    blk_start = jnp.arange(n_blocks, dtype=jnp.int32) * rb
    block_e = jnp.minimum(jnp.sum(pad_end[None, :] <= blk_start[:, None], axis=1),
                          N_EXPERTS - 1).astype(jnp.int32)
    n_used = (pad_end[N_EXPERTS - 1:] // rb).astype(jnp.int32)
    return pad_end, dests, block_e, n_used, n_blocks * rb


def _dest_tiles(dest, t):
    T = dest.shape[1]
    return dest.reshape(TOP_K, T // t, t).transpose(1, 0, 2).reshape(T // t, 1, TOP_K * t)


def _combine_kernel(dcur_ref, dnext_ref, x1_ref, ew_ref, gf_ref, yb_hbm, y_ref, ybuf, sems,
                    *, tc, n_tiles):
    i = pl.program_id(0)

    def copies(dref, slot, r):
        return [pltpu.make_async_copy(yb_hbm.at[pl.ds(dref[0, 0, k * tc + r], 1), :],
                                      ybuf.at[slot, k, pl.ds(r, 1), :], sems.at[slot])
                for k in range(TOP_K)]

    def gather(dref, slot):
        for r in range(tc):
            for k, cp in enumerate(copies(dref, slot, r)):
                cp.start(priority=k)

    def drain(dref, slot):
        for r in range(tc):
            for cp in copies(dref, slot, r):
                cp.wait()

    pl.when(i == 0)(lambda: gather(dcur_ref, 0))

    def step(slot):
        drain(dcur_ref, slot)
        gather(dnext_ref, 1 - slot)
        w = ew_ref[...]
        moe = w[:, 0:1] * ybuf[slot, 0] + w[:, 1:2] * ybuf[slot, 1]
        x2 = x1_ref[...] + moe
        y_ref[...] = x2 * _rms_scale(x2) * gf_ref[...]
        pl.when(i == n_tiles - 1)(lambda: drain(dnext_ref, 1 - slot))

    pl.when(lax.rem(i, 2) == 0)(lambda: step(0))
    pl.when(lax.rem(i, 2) == 1)(lambda: step(1))


def _combine(dest_tiles, x1, ew, final_g, yb, tc):
    T = x1.shape[0]
    n_tiles = T // tc
    tile = lambda i: (i, 0)
    dspec = lambda f: pl.BlockSpec((1, 1, TOP_K * tc), f, memory_space=pltpu.SMEM)
    return pl.pallas_call(
        functools.partial(_combine_kernel, tc=tc, n_tiles=n_tiles), grid=(n_tiles,),
        in_specs=[dspec(lambda i: (i, 0, 0)),
                  dspec(lambda i: (jnp.minimum(i + 1, n_tiles - 1), 0, 0)),
                  pl.BlockSpec((tc, D_MODEL), tile), pl.BlockSpec((tc, LANES), tile),
                  pl.BlockSpec((1, D_MODEL), lambda i: (0, 0)),
                  pl.BlockSpec(memory_space=pl.ANY)],
        out_specs=pl.BlockSpec((tc, D_MODEL), tile),
        out_shape=jax.ShapeDtypeStruct((T, D_MODEL), F32),
        scratch_shapes=[pltpu.VMEM((2, TOP_K, tc, D_MODEL), F32), pltpu.SemaphoreType.DMA((2,))],
        compiler_params=_params(("arbitrary",)), name="combine",
    )(dest_tiles, dest_tiles, x1, ew, final_g, yb)


def _prep_weights(ln1_g, ln2_g, w_in, lru_conv_w, lru_conv_b, lru_wa, lru_ba, lru_wx, lru_bx,
                  lru_lambda, gdn_conv_w, gdn_a_log, gdn_dt_bias, gdn_norm_g, w_out,
                  router_group_w, router_group_b, router_expert_w, router_expert_b,
                  moe_w1, moe_w3, moe_w2, final_g):
    w_in0 = w_in[0]
    half = LRU_WIDTH // 2
    per_half = half // LRU_BLOCK

    def block_diag(w):
        w4 = w.reshape(2, per_half, LRU_BLOCK, LRU_BLOCK)
        eye = jnp.eye(per_half, dtype=w.dtype)
        return jnp.einsum("snij,nm->snimj", w4, eye).reshape(2, half, half)

    wgate = jnp.concatenate([block_diag(lru_wa[0]), block_diag(lru_wx[0])], axis=2).astype(BF16)
    zeros4 = jnp.zeros((GDN_HEADS,), F32)
    neg_a = jnp.concatenate([zeros4, jnp.exp(gdn_a_log[0].astype(F32))])
    dtb = jnp.concatenate([zeros4, gdn_dt_bias[0].astype(F32)])
    gate_par = jnp.stack([neg_a, dtb])
    w_router = jnp.zeros((D_MODEL, LANES), F32)
    w_router = w_router.at[:, :N_GROUPS].set(router_group_w[0])
    w_router = w_router.at[:, N_GROUPS:N_GROUPS + N_EXPERTS].set(router_expert_w[0])
    b_router = jnp.zeros((1, LANES), F32)
    b_router = b_router.at[0, :N_GROUPS].set(router_group_b[0])
    b_router = b_router.at[0, N_GROUPS:N_GROUPS + N_EXPERTS].set(router_expert_b[0])
    return {
        "ln1_g": ln1_g[0].reshape(1, D_MODEL), "ln2_g": ln2_g[0].reshape(1, D_MODEL),
        "w_main": w_in0.astype(BF16),
        "w_gate_col": w_in0[:, N_MAIN:].astype(BF16),
        "w_gate_row": w_in0[:, N_MAIN:].T.astype(BF16),
        "lru_cw": lru_conv_w[0], "lru_cb": lru_conv_b[0].reshape(1, LRU_WIDTH),
        "lru_wgate": wgate,
        "lru_ba": lru_ba[0].reshape(1, LRU_WIDTH), "lru_bx": lru_bx[0].reshape(1, LRU_WIDTH),
        "lru_lam": lru_lambda[0].reshape(1, LRU_WIDTH),
        "gdn_cw": gdn_conv_w[0], "gate_par_col": gate_par, "gate_par_row": gate_par.T,
        "gdn_ng": gdn_norm_g[0].reshape(1, GDN_DV),
        "w_out": w_out[0].astype(BF16),
        "w_router": w_router, "b_router": b_router,
        "moe_w1": moe_w1[0], "moe_w3": moe_w3[0], "moe_w2": moe_w2[0],
        "final_g": final_g.reshape(1, D_MODEL),
    }


def _pad_conv_state(s):
    return jnp.pad(s, ((0, 0), (SUBLANES - (CONV_W - 1), 0), (0, 0)))


def _mixer(x, conv_l, h_l, conv_g, s_g, cnt0, wts, tl_in, bb, cps, tl_tok):
    B, L, _ = x.shape
    T = B * L
    c = min(CHUNK, L)
    oa, q, k, v, gz, gcol, grow, nlc, nlh, ngc = _mixer_in(
        x, _pad_conv_state(conv_l), h_l.reshape(B, 1, LRU_WIDTH), _pad_conv_state(conv_g), wts, tl_in)
    og, ns = _gdn(q, k, v, gz, gcol, grow, s_g, wts["gdn_ng"], bb, c, cps)
    routed = _mixer_out(oa.reshape(T, LRU_WIDTH), og.reshape(T, GDN_V), x.reshape(T, D_MODEL),
                        cnt0, wts, tl_tok)
    return routed, (nlc[None], nlh.reshape(1, B, LRU_WIDTH), ngc[None], ns[None])


def kernel(x_prompt, x_sample, state_lru_conv, state_lru_h, state_gdn_conv, state_gdn_S, ln1_g, ln2_g, w_in, lru_conv_w, lru_conv_b, lru_wa, lru_ba, lru_wx, lru_bx, lru_lambda, gdn_conv_w, gdn_a_log, gdn_dt_bias, gdn_norm_g, w_out, router_group_w, router_group_b, router_expert_w, router_expert_b, moe_w1, moe_w3, moe_w2, final_g):
    wts = _prep_weights(ln1_g, ln2_g, w_in, lru_conv_w, lru_conv_b, lru_wa, lru_ba, lru_wx, lru_bx,
                        lru_lambda, gdn_conv_w, gdn_a_log, gdn_dt_bias, gdn_norm_g, w_out,
                        router_group_w, router_group_b, router_expert_w, router_expert_b,
                        moe_w1, moe_w3, moe_w2, final_g)
    B, L, _ = x_prompt.shape
    Bs, Ls, _ = x_sample.shape
    dt = x_prompt.dtype
    tile_p, tile_s = 512, Bs * Ls
    row_block = 512
    routed_p, states_p = _mixer(
        x_prompt,
        jnp.zeros((B, CONV_W - 1, LRU_WIDTH), dt), jnp.zeros((B, LRU_WIDTH), dt),
        jnp.zeros((B, CONV_W - 1, GDN_CONV_CH), dt), jnp.zeros((B, GDN_HEADS, GDN_DK, GDN_DV), dt),
        jnp.zeros((1, LANES), F32), wts, tl_in=512, bb=4, cps=4, tl_tok=tile_p)
    x1_p, h2p_p, ew_p, ei_p, cnt_p = routed_p
    routed_s, states_s = _mixer(
        x_sample, state_lru_conv[0], state_lru_h[0], state_gdn_conv[0], state_gdn_S[0],
        cnt_p, wts, tl_in=Ls, bb=2, cps=1, tl_tok=tile_s)
    x1_s, h2p_s, ew_s, ei_s, cnt = routed_s

    pad_end, (dest_p, dest_s), block_e, n_used, n_rows = _dispatch_plan([ei_p, ei_s], cnt, row_block)
    xs = _dispatch(pad_end, _dest_tiles(dest_p, tile_p), h2p_p, n_rows, tile_p, row_block)
    xs = _dispatch(pad_end, _dest_tiles(dest_s, tile_s), h2p_s, n_rows, tile_s, row_block, xs_prev=xs)
    yb = _moe(block_e, n_used, xs, wts, row_block)
    y_p = _combine(_dest_tiles(dest_p, tile_p), x1_p, ew_p, wts["final_g"], yb, tile_p)
    y_s = _combine(_dest_tiles(dest_s, tile_s), x1_s, ew_s, wts["final_g"], yb, tile_s)
    return (y_p.reshape(B, L, D_MODEL), y_s.reshape(Bs, Ls, D_MODEL)) + states_p + states_s
```
